```python
import math
import jax, jax.numpy as jnp
from jax import lax
import numpy as np

D_MODEL = 1024
BATCH = 8
SEQ = 8192
DEPTH = 4

GROUP_WIDTH = D_MODEL // 4
HEAD_DIM = 64
N_FOX_HEADS = GROUP_WIDTH // HEAD_DIM
N_DIL_HEADS = GROUP_WIDTH // HEAD_DIM
N_DIFF_HEADS = GROUP_WIDTH // HEAD_DIM
DIFF_HALF = HEAD_DIM // 2
S5_CH = 16
S5_GROUPS = GROUP_WIDTH // S5_CH
S5_STATE = 64
BLOCK = 128
DIL_PAIRS = ((128, 1), (512, 4), (2048, 16))
REL_BUCKETS = 32
REL_MAX_EXACT = REL_BUCKETS // 2
REL_MAX_DIST = 2048
N_KEYS = 128
N_EXPERTS = N_KEYS * N_KEYS
PEER_HEADS = 8
PEER_TOPK = 16
PEER_QDIM = 256
PEER_CHUNK = 128
RMS_EPS = 1e-6
SPLIT_SIZES = [GROUP_WIDTH, GROUP_WIDTH, GROUP_WIDTH, N_FOX_HEADS,
               GROUP_WIDTH, GROUP_WIDTH, GROUP_WIDTH,
               GROUP_WIDTH, GROUP_WIDTH, GROUP_WIDTH,
               GROUP_WIDTH]
SPLIT_POINTS = [int(v) for v in np.cumsum(SPLIT_SIZES)[:-1]]
IN_WIDTH = int(sum(SPLIT_SIZES))

kernel_name = "hybrid_fox_dilated_diff_s5_peer"


def rms_norm(x, g):
    xf = x.astype(jnp.float32)
    y = xf * lax.rsqrt(jnp.mean(xf * xf, axis=-1, keepdims=True) + RMS_EPS)
    return (y * g.astype(jnp.float32)).astype(x.dtype)


def heads(t, n):
    b, s, _ = t.shape
    return t.reshape(b, s, n, -1).transpose(0, 2, 1, 3)


def to_blocks(t):
    b, h, s = t.shape[:3]
    t = t.reshape(b, h, s // BLOCK, BLOCK, *t.shape[3:])
    return jnp.moveaxis(t, 2, 0)


def unblock(o):
    nb, b, h, w, d = o.shape
    return o.transpose(1, 0, 3, 2, 4).reshape(b, nb * w, h * d)


def t5_bucket(dist):
    n = jnp.maximum(dist, 0)
    nf = jnp.maximum(n, REL_MAX_EXACT).astype(jnp.float32)
    large = REL_MAX_EXACT + (jnp.log(nf / REL_MAX_EXACT) / math.log(REL_MAX_DIST / REL_MAX_EXACT)
                             * (REL_BUCKETS - REL_MAX_EXACT)).astype(jnp.int32)
    large = jnp.minimum(large, REL_BUCKETS - 1)
    return jnp.where(n < REL_MAX_EXACT, n, large)


def forgetting_attention(q, k, v, f_logit):
    b, h, s, d = q.shape
    cum = jnp.cumsum(jax.nn.log_sigmoid(f_logit.astype(jnp.float32)), axis=1).transpose(0, 2, 1)
    scale = d ** -0.5
    kpos = jnp.arange(s)

    def block(args):
        qb, cq, q0 = args
        logits = jnp.einsum('bhqd,bhkd->bhqk', qb, k, preferred_element_type=jnp.float32) * scale
        logits = logits + (cq[..., :, None] - cum[..., None, :])
        qpos = q0 + jnp.arange(BLOCK)
        logits = jnp.where(kpos[None, :] <= qpos[:, None], logits, -jnp.inf)
        p = jax.nn.softmax(logits, axis=-1)
        return jnp.einsum('bhqk,bhkd->bhqd', p.astype(v.dtype), v)

    nb = s // BLOCK
    o = lax.map(block, (to_blocks(q), to_blocks(cum), jnp.arange(nb) * BLOCK))
    return unblock(o)


def dilated_branch(q, k, v, bias_tab, window, dil):
    b, h, s, d = q.shape
    ws = window // dil
    L = s // dil
    nb = -(-L // BLOCK)
    Lp = nb * BLOCK

    def strided(t):
        t = t.reshape(b, h, L, dil, d).transpose(0, 1, 3, 2, 4)
        t = jnp.pad(t, ((0, 0), (0, 0), (0, 0), (0, Lp - L), (0, 0)))
        return t.reshape(b, h, dil, nb, BLOCK, d)

    def with_prev(t):
        prev = jnp.pad(t, ((0, 0), (0, 0), (0, 0), (1, 0), (0, 0), (0, 0)))[:, :, :, :nb]
        return jnp.concatenate([prev, t], axis=-2)

    qs = strided(q)
    kk = with_prev(strided(k))
    vv = with_prev(strided(v))
    j = jnp.arange(BLOCK)[:, None]
    m = jnp.arange(2 * BLOCK)[None, :]
    sub_dist = BLOCK + j - m
    band = (sub_dist >= 0) & (sub_dist <= ws)
    blk = jnp.arange(nb)[:, None, None]
    valid = band[None] & (blk * BLOCK + m[None] - BLOCK >= 0)
    bias = bias_tab[t5_bucket(dil * jnp.clip(sub_dist, 0, ws))].transpose(2, 0, 1)
    logits = jnp.einsum('bhrnqd,bhrnkd->bhrnqk', qs, kk, preferred_element_type=jnp.float32) * (d ** -0.5)
    logits = logits + bias[:, None, None].astype(jnp.float32)
    logits = jnp.where(valid, logits, -jnp.inf)
    mx = jnp.max(logits, axis=-1, keepdims=True)
    e = jnp.exp(logits - mx)
    den = jnp.sum(e, axis=-1, keepdims=True)
    o = jnp.einsum('bhrnqk,bhrnkd->bhrnqd', (e / den).astype(v.dtype), vv)
    lse = (mx + jnp.log(den))[..., 0]

    def unstrided(t):
        rest = t.shape[5:]
        t = t.reshape(b, h, dil, Lp, *rest)[:, :, :, :L]
        t = jnp.moveaxis(t, 2, 3)
        return t.reshape(b, h, s, *rest)

    return unstrided(o), unstrided(lse)


def dilated_attention(q, k, v, bias_tab):
    b, h, s, d = q.shape
    branches = [dilated_branch(q, k, v, bias_tab, w, r) for (w, r) in DIL_PAIRS]
    outs = jnp.stack([o for o, _ in branches])
    wts = jax.nn.softmax(jnp.stack([l for _, l in branches]), axis=0)
    o = jnp.einsum('nbhs,nbhsd->bhsd', wts.astype(v.dtype), outs)
    return o.transpose(0, 2, 1, 3).reshape(b, s, h * d)


def diff_attention(q, k, v, bias_tab, lam, out_gain, lam_init):
    b, h, s = q.shape[:3]
    scale = DIFF_HALF ** -0.5
    kpos = jnp.arange(s)

    def block(args):
        qb, q0 = args
        qpos = q0 + jnp.arange(BLOCK)
        rel = qpos[:, None] - kpos[None, :]
        bias = bias_tab[t5_bucket(rel)].transpose(2, 0, 1).astype(jnp.float32)
        logits = jnp.einsum('bhqcd,bhkcd->cbhqk', qb, k, preferred_element_type=jnp.float32) * scale + bias
        logits = jnp.where(rel >= 0, logits, -jnp.inf)
        p = jax.nn.softmax(logits, axis=-1)
        attn = p[0] - lam * p[1]
        o = jnp.einsum('bhqk,bhkd->bhqd', attn.astype(v.dtype), v)
        return rms_norm(o, out_gain) * (1.0 - lam_init)

    nb = s // BLOCK
    o = lax.map(block, (to_blocks(q), jnp.arange(nb) * BLOCK))
    return unblock(o)


def s5_mixer(u, a_re, a_im, log_dt, b_re, b_im, c_re, c_im, d_skip, glu_w, glu_b):
    bsz, s, _ = u.shape
    uf = u.astype(jnp.float32).reshape(bsz, s, S5_GROUPS, S5_CH)
    dt = jnp.exp(log_dt.astype(jnp.float32))[:, None]
    ar, ai = a_re.astype(jnp.float32), a_im.astype(jnp.float32)
    mag = jnp.exp(dt * ar)
    abar_re, abar_im = mag * jnp.cos(dt * ai), mag * jnp.sin(dt * ai)
    den = ar * ar + ai * ai
    nr, ni = abar_re - 1.0, abar_im
    coef_re = (nr * ar + ni * ai) / den
    coef_im = (ni * ar - nr * ai) / den
    br, bi = b_re.astype(jnp.float32), b_im.astype(jnp.float32)
    bbar_re = coef_re[..., None] * br - coef_im[..., None] * bi
    bbar_im = coef_re[..., None] * bi + coef_im[..., None] * br
    bu_re = jnp.einsum('gnc,bsgc->bsgn', bbar_re, uf)
    bu_im = jnp.einsum('gnc,bsgc->bsgn', bbar_im, uf)
    a_full_re = jnp.broadcast_to(abar_re, bu_re.shape)
    a_full_im = jnp.broadcast_to(abar_im, bu_re.shape)

    def combine(e1, e2):
        a1r, a1i, b1r, b1i = e1
        a2r, a2i, b2r, b2i = e2
        return (a2r * a1r - a2i * a1i, a2r * a1i + a2i * a1r,
                a2r * b1r - a2i * b1i + b2r, a2r * b1i + a2i * b1r + b2i)

    _, _, xr, xi = lax.associative_scan(combine, (a_full_re, a_full_im, bu_re, bu_im), axis=1)
    y = (jnp.einsum('gcn,bsgn->bsgc', c_re.astype(jnp.float32), xr)
         - jnp.einsum('gcn,bsgn->bsgc', c_im.astype(jnp.float32), xi)
         + d_skip.astype(jnp.float32) * uf)
    g = jax.nn.gelu(y.reshape(bsz, s, -1))
    out = g * jax.nn.sigmoid(g @ glu_w.astype(jnp.float32) + glu_b.astype(jnp.float32))
    return out.astype(u.dtype)


def peer(h, wq, keys, u_tab, v_tab):
    b, s, dm = h.shape
    chunks = h.reshape(-1, PEER_CHUNK, dm)

    def chunk(xc):
        c = xc.shape[0]
        q = (xc @ wq).reshape(c, PEER_HEADS, 2, PEER_QDIM // 2)
        sc = jnp.einsum('thpd,hpkd->thpk', q, keys, preferred_element_type=jnp.float32)
        top_v, top_i = lax.top_k(sc, PEER_TOPK)
        cand = (top_v[:, :, 0, :, None] + top_v[:, :, 1, None, :]).reshape(c, PEER_HEADS, -1)
        cand_id = (top_i[:, :, 0, :, None] * N_KEYS + top_i[:, :, 1, None, :]).reshape(c, PEER_HEADS, -1)
        best_v, best_pos = lax.top_k(cand, PEER_TOPK)
        ids = jnp.take_along_axis(cand_id, best_pos, axis=-1)
        gate = jax.nn.softmax(best_v, axis=-1)
        ug = u_tab[ids]
        act = jax.nn.gelu(jnp.einsum('td,thkd->thk', xc, ug, preferred_element_type=jnp.float32)) * gate
        vg = v_tab[ids]
        return jnp.einsum('thk,thkd->td', act.astype(vg.dtype), vg)

    return lax.map(chunk, chunks).reshape(b, s, dm)


def setup_inputs(seed: int = 0) -> dict:
    key = jax.random.key(seed)
    ks = jax.random.split(key, 32)
    f32 = jnp.float32
    L, D, G, N, C = DEPTH, D_MODEL, S5_GROUPS, S5_STATE, S5_CH

    def nrm(k, shape, scale):
        return jax.random.normal(k, shape, f32) * scale

    def gain(k, shape):
        return 1.0 + 0.02 * jax.random.normal(k, shape, f32)

    return {
        "x": jax.random.normal(ks[0], (BATCH, SEQ, D), f32),
        "w_in": nrm(ks[1], (L, D, IN_WIDTH), D ** -0.5),
        "b_forget": jax.random.uniform(ks[2], (L, N_FOX_HEADS), f32, 1.0, 6.0),
        "w_out": nrm(ks[3], (L, D, D), D ** -0.5),
        "norm_mix": gain(ks[4], (L, D)),
        "norm_ffn": gain(ks[5], (L, D)),
        "fox_q_norm": gain(ks[6], (L, HEAD_DIM)),
        "fox_k_norm": gain(ks[7], (L, HEAD_DIM)),
        "dil_q_norm": gain(ks[8], (L, HEAD_DIM)),
        "dil_k_norm": gain(ks[9], (L, HEAD_DIM)),
        "diff_q_norm": gain(ks[10], (L, DIFF_HALF)),
        "diff_k_norm": gain(ks[11], (L, DIFF_HALF)),
        "diff_lambda": nrm(ks[12], (L, 4, DIFF_HALF), 0.1),
        "diff_out_norm": gain(ks[13], (L, HEAD_DIM)),
        "rel_bias": nrm(ks[14], (REL_BUCKETS, N_DIL_HEADS + N_DIFF_HEADS), 0.2),
        "s5_a_re": -0.5 + 0.01 * jax.random.normal(ks[15], (L, G, N), f32),
        "s5_a_im": math.pi * jnp.arange(N, dtype=f32)[None, None, :] + 0.01 * jax.random.normal(ks[16], (L, G, N), f32),
        "s5_log_dt": jax.random.uniform(ks[17], (L, G), f32, math.log(1e-3), math.log(1e-1)),
        "s5_b_re": nrm(ks[18], (L, G, N, C), (2 * C) ** -0.5),
        "s5_b_im": nrm(ks[19], (L, G, N, C), (2 * C) ** -0.5),
        "s5_c_re": nrm(ks[20], (L, G, C, N), N ** -0.5),
        "s5_c_im": nrm(ks[21], (L, G, C, N), N ** -0.5),
        "s5_d": nrm(ks[22], (L, G, C), 1.0),
        "s5_glu_w": nrm(ks[23], (L, GROUP_WIDTH, GROUP_WIDTH), GROUP_WIDTH ** -0.5),
        "s5_glu_b": nrm(ks[24], (L, GROUP_WIDTH), 0.01),
        "peer_wq": nrm(ks[25], (L, D, PEER_HEADS * PEER_QDIM), D ** -0.5),
        "peer_keys": nrm(ks[26], (L, PEER_HEADS, 2, N_KEYS, PEER_QDIM // 2), (PEER_QDIM // 2) ** -0.5),
        "peer_u": nrm(ks[27], (L, N_EXPERTS, D), D ** -0.5),
        "peer_v": nrm(ks[28], (L, N_EXPERTS, D), PEER_TOPK ** -0.5),
    }


def reference(x, w_in, b_forget, w_out, norm_mix, norm_ffn, fox_q_norm, fox_k_norm,
              dil_q_norm, dil_k_norm, diff_q_norm, diff_k_norm, diff_lambda, diff_out_norm,
              rel_bias, s5_a_re, s5_a_im, s5_log_dt, s5_b_re, s5_b_im, s5_c_re, s5_c_im,
              s5_d, s5_glu_w, s5_glu_b, peer_wq, peer_keys, peer_u, peer_v):
    b, s, _ = x.shape
    for layer in range(DEPTH):
        h = rms_norm(x, norm_mix[layer])
        z = h @ w_in[layer]
        fq, fk, fv, ff, dq, dk, dv, cq, ck, cv, su = jnp.split(z, SPLIT_POINTS, axis=-1)
        fox_o = forgetting_attention(rms_norm(heads(fq, N_FOX_HEADS), fox_q_norm[layer]),
                                     rms_norm(heads(fk, N_FOX_HEADS), fox_k_norm[layer]),
                                     heads(fv, N_FOX_HEADS), ff + b_forget[layer])
        dil_o = dilated_attention(rms_norm(heads(dq, N_DIL_HEADS), dil_q_norm[layer]),
                                  rms_norm(heads(dk, N_DIL_HEADS), dil_k_norm[layer]),
                                  heads(dv, N_DIL_HEADS), rel_bias[:, :N_DIL_HEADS])
        cqh = rms_norm(heads(cq, N_DIFF_HEADS).reshape(b, N_DIFF_HEADS, s, 2, DIFF_HALF), diff_q_norm[layer])
        ckh = rms_norm(heads(ck, N_DIFF_HEADS).reshape(b, N_DIFF_HEADS, s, 2, DIFF_HALF), diff_k_norm[layer])
        lam_init = 0.8 - 0.6 * math.exp(-0.3 * layer)
        lam_p = diff_lambda[layer].astype(jnp.float32)
        lam = jnp.exp(jnp.sum(lam_p[0] * lam_p[1])) - jnp.exp(jnp.sum(lam_p[2] * lam_p[3])) + lam_init
        diff_o = diff_attention(cqh, ckh, heads(cv, N_DIFF_HEADS), rel_bias[:, N_DIL_HEADS:],
                                lam, diff_out_norm[layer], lam_init)
        s5_o = s5_mixer(su, s5_a_re[layer], s5_a_im[layer], s5_log_dt[layer], s5_b_re[layer],
                        s5_b_im[layer], s5_c_re[layer], s5_c_im[layer], s5_d[layer],
                        s5_glu_w[layer], s5_glu_b[layer])
        mixed = jnp.concatenate([fox_o, dil_o, diff_o, s5_o], axis=-1)
        x = x + mixed @ w_out[layer]
        x = x + peer(rms_norm(x, norm_ffn[layer]), peer_wq[layer], peer_keys[layer],
                     peer_u[layer], peer_v[layer])
    return x
```

```python
import functools
import math

import numpy as np
import jax
import jax.numpy as jnp
from jax import lax
from jax.experimental import pallas as pl
from jax.experimental.pallas import tpu as pltpu

F32 = jnp.float32
BF16 = jnp.bfloat16

LANES = 128
GROUP = 256
HEAD_DIM = 64
N_HEADS = GROUP // HEAD_DIM
DIFF_HALF = HEAD_DIM // 2
S5_CH = 16
S5_GROUPS = GROUP // S5_CH
S5_STATE = 64
S5_WIDE = S5_GROUPS * S5_STATE
DIL_BLOCK = 128
DIL_PAIRS = ((128, 1), (512, 4), (2048, 16))
REL_BUCKETS = 32
REL_MAX_EXACT = REL_BUCKETS // 2
REL_MAX_DIST = 2048
N_KEYS = 128
PEER_HEADS = 8
PEER_TOPK = 16
RMS_EPS = 1e-6
NEG = -1e30
VMEM_LIMIT = 56 * 1024 * 1024

IN_COLS = 10 * GROUP + LANES
FORGET_COL = 10 * GROUP


def _cparams(sem):
    return pltpu.CompilerParams(dimension_semantics=sem, vmem_limit_bytes=VMEM_LIMIT)


def _dot(a, b):
    return jnp.dot(a, b, preferred_element_type=F32)


def _dot_nt(a, b):
    return lax.dot_general(a, b, (((1,), (1,)), ((), ())), preferred_element_type=F32)


def _split3(x):
    hi = x.astype(BF16)
    r = x - hi.astype(F32)
    mid = r.astype(BF16)
    lo = (r - mid.astype(F32)).astype(BF16)
    return hi, mid, lo


def _gelu(x):
    return 0.5 * x * (1.0 + jnp.tanh(math.sqrt(2.0 / math.pi) * (x + 0.044715 * (x * x * x))))


def _t5_bucket(dist):
    n = jnp.maximum(dist, 0)
    nf = jnp.maximum(n, REL_MAX_EXACT).astype(F32)
    large = REL_MAX_EXACT + (jnp.log(nf / REL_MAX_EXACT) / math.log(REL_MAX_DIST / REL_MAX_EXACT)
                             * (REL_BUCKETS - REL_MAX_EXACT)).astype(jnp.int32)
    large = jnp.minimum(large, REL_BUCKETS - 1)
    return jnp.where(n < REL_MAX_EXACT, n, large)


def _bucket_saturation():
    n = REL_MAX_EXACT * (REL_MAX_DIST / REL_MAX_EXACT) ** ((REL_BUCKETS - 1 - REL_MAX_EXACT)
                                                            / (REL_BUCKETS - REL_MAX_EXACT))
    return int(math.ceil(n)) + 2


def _inproj_kernel(x_ref, g_ref, w_ref, bf_ref, gains_ref, ones64_ref, ones32_ref, tri_ref, aug_ref,
                   fq_ref, fk_ref, fv_ref, cq1_ref, cq2_ref, ck_ref, cv_ref,
                   dq_ref, dk_ref, dv_ref, su_ref, carry_ref):
    @pl.when(pl.program_id(1) == 0)
    def _():
        carry_ref[...] = jnp.zeros_like(carry_ref)

    x = x_ref[0]
    tm = x.shape[0]
    h = x * lax.rsqrt(jnp.mean(x * x, axis=-1, keepdims=True) + RMS_EPS) * g_ref[...]
    z = _dot(h.astype(BF16), w_ref[...])

    def group(idx):
        return z[:, idx * GROUP:(idx + 1) * GROUP]

    def head_norm(zg, ones_ref, width, gain_row):
        ss = _dot((zg * zg).astype(BF16), ones_ref[...])
        return zg * lax.rsqrt(ss * (1.0 / width) + RMS_EPS) * gains_ref[gain_row:gain_row + 1, :]

    lane = lax.broadcasted_iota(jnp.int32, (tm, LANES), 1)

    def head_chunk(zg, hh):
        c = zg[:, (hh // 2) * LANES:(hh // 2 + 1) * LANES]
        if hh % 2:
            c = pltpu.roll(c, HEAD_DIM, axis=1)
        return c

    f = z[:, FORGET_COL:FORGET_COL + LANES] + bf_ref[...]
    ls = jnp.minimum(f, 0.0) - jnp.log1p(jnp.exp(-jnp.abs(f)))
    p0, p1, p2 = _split3(ls)
    tri = tri_ref[...]
    cum = (_dot(tri, p0) + _dot(tri, p1)) + _dot(tri, p2) + carry_ref[...]
    carry_ref[...] = cum[tm - 1:tm, :]

    fq = head_norm(group(0), ones64_ref, HEAD_DIM, 0)
    fk = head_norm(group(1), ones64_ref, HEAD_DIM, 1)
    fv = group(2)
    dq = head_norm(group(3), ones64_ref, HEAD_DIM, 2)
    dk = head_norm(group(4), ones64_ref, HEAD_DIM, 3)
    cq = head_norm(group(6), ones32_ref, DIFF_HALF, 4)
    ck = head_norm(group(7), ones32_ref, DIFF_HALF, 5)
    cv = group(8)

    dq_ref[0] = dq.astype(BF16)
    dk_ref[0] = dk.astype(BF16)
    dv_ref[0] = group(5).astype(BF16)
    su_ref[...] = group(9)

    one = jnp.ones((tm, LANES), F32)
    zero = jnp.zeros((tm, LANES), F32)
    for hh in range(N_HEADS):
        c = jnp.broadcast_to(cum[:, hh:hh + 1], (tm, LANES))
        c0, c1, c2 = _split3(c)
        c0, c1, c2 = c0.astype(F32), c1.astype(F32), c2.astype(F32)
        qa = jnp.where(lane == 64, c0, jnp.where(lane == 65, c1, jnp.where(lane == 66, c2,
             jnp.where(lane < 70, one, zero))))
        ka = jnp.where(lane == 67, -c0, jnp.where(lane == 68, -c1, jnp.where(lane == 69, -c2,
             jnp.where(lane < 70, one, zero))))
        fq_ref[0, hh] = jnp.where(lane < HEAD_DIM, head_chunk(fq, hh), qa).astype(BF16)
        fk_ref[0, hh] = jnp.where(lane < HEAD_DIM, head_chunk(fk, hh), ka).astype(BF16)
        vaug = aug_ref[0:1, :]
        fv_ref[0, hh] = jnp.where(lane < HEAD_DIM, head_chunk(fv, hh), vaug).astype(BF16)
        qaug = aug_ref[1:2, :]
        cqc = head_chunk(cq, hh)
        cq1_ref[0, hh] = jnp.where(lane < DIFF_HALF, cqc, qaug).astype(BF16)
        cq2_ref[0, hh] = jnp.where((lane >= DIFF_HALF) & (lane < HEAD_DIM), cqc, qaug).astype(BF16)
        kaug = aug_ref[2 + hh:3 + hh, :]
        ck_ref[0, hh] = jnp.where(lane < HEAD_DIM, head_chunk(ck, hh), kaug).astype(BF16)
        cv_ref[0, hh] = jnp.where(lane < HEAD_DIM, head_chunk(cv, hh), vaug).astype(BF16)


def _inproj(x, g, w, bf, gains, ones64, ones32, tri, aug, tm):
    B, S, D = x.shape
    nt = S // tm
    hm = jax.ShapeDtypeStruct((B, N_HEADS, S, LANES), BF16)
    nat = jax.ShapeDtypeStruct((B, S, GROUP), BF16)
    hm_spec = pl.BlockSpec((1, N_HEADS, tm, LANES), lambda b, i: (b, 0, i, 0))
    nat_spec = pl.BlockSpec((1, tm, GROUP), lambda b, i: (b, i, 0))

    def full(a):
        return pl.BlockSpec(a.shape, lambda b, i: (0,) * a.ndim)

    return pl.pallas_call(
        _inproj_kernel,
        grid=(B, nt),
        in_specs=[pl.BlockSpec((1, tm, D), lambda b, i: (b, i, 0)),
                  full(g), full(w), full(bf), full(gains), full(ones64), full(ones32), full(tri), full(aug)],
        out_specs=[hm_spec] * 7 + [nat_spec] * 3 + [pl.BlockSpec((tm, GROUP), lambda b, i: (i, b))],
        out_shape=[hm] * 7 + [nat] * 3 + [jax.ShapeDtypeStruct((S, B * GROUP), F32)],
        scratch_shapes=[pltpu.VMEM((1, LANES), F32)],
        compiler_params=_cparams(("arbitrary", "arbitrary")),
        name="inproj",
    )(x, g, w, bf, gains, ones64, ones32, tri, aug)


def _flash_step(q, k, v, m, acc, bias=None):
    s = _dot_nt(q, k)
    if bias is not None:
        s = s + bias
    m_new = jnp.maximum(m, jnp.max(s, axis=1, keepdims=True))
    alpha = jnp.exp(m - m_new)
    p = jnp.exp(s - m_new)
    acc = alpha * acc + _dot(p.astype(BF16), v)
    return m_new, acc


def _fox_kernel(q_ref, k_ref, v_ref, o_ref, *, tq, tk):
    i = pl.program_id(2)
    q = q_ref[0, 0]
    n_full = (i * tq) // tk

    def body(j, carry):
        m, acc = carry
        off = pl.multiple_of(j * tk, tk)
        return _flash_step(q, k_ref[0, 0, pl.ds(off, tk), :], v_ref[0, 0, pl.ds(off, tk), :], m, acc)

    m0 = jnp.full((tq, 1), NEG, F32)
    acc0 = jnp.zeros((tq, LANES), F32)
    m, acc = lax.fori_loop(0, n_full, body, (m0, acc0))
    off = pl.multiple_of(n_full * tk, tk)
    qpos = i * tq + lax.broadcasted_iota(jnp.int32, (tq, tk), 0)
    kpos = off + lax.broadcasted_iota(jnp.int32, (tq, tk), 1)
    mask = jnp.where(kpos <= qpos, 0.0, NEG)
    m, acc = _flash_step(q, k_ref[0, 0, pl.ds(off, tk), :], v_ref[0, 0, pl.ds(off, tk), :], m, acc, mask)
    den = jnp.broadcast_to(acc[:, HEAD_DIM:HEAD_DIM + 1], (tq, LANES))
    o_ref[0, 0] = (acc / den).astype(BF16)


def _fox_attention(q, k, v, tq, tk):
    B, H, S, _ = q.shape
    return pl.pallas_call(
        functools.partial(_fox_kernel, tq=tq, tk=tk),
        grid=(B, H, S // tq),
        in_specs=[pl.BlockSpec((1, 1, tq, LANES), lambda b, h, i: (b, h, i, 0)),
                  pl.BlockSpec((1, 1, S, LANES), lambda b, h, i: (b, h, 0, 0)),
                  pl.BlockSpec((1, 1, S, LANES), lambda b, h, i: (b, h, 0, 0))],
        out_specs=pl.BlockSpec((1, 1, tq, LANES), lambda b, h, i: (b, h, i, 0)),
        out_shape=jax.ShapeDtypeStruct((B, H, S, LANES), BF16),
        compiler_params=_cparams(("arbitrary", "arbitrary", "arbitrary")),
        name="fox_attention",
    )(q, k, v)


def _diff_kernel(lam_ref, q1_ref, q2_ref, k_ref, v_ref, bias_ref, gain_ref, o_ref, *, t, n_near, post):
    i = pl.program_id(2)
    q1 = q1_ref[0, 0]
    q2 = q2_ref[0, 0]
    n_far = jnp.maximum(i - (n_near - 1), 0)

    def kv(j):
        off = pl.multiple_of(j * t, t)
        return k_ref[0, 0, pl.ds(off, t), :], v_ref[0, 0, pl.ds(off, t), :]

    def far(j, carry):
        m1, a1, m2, a2 = carry
        k, v = kv(j)
        m1, a1 = _flash_step(q1, k, v, m1, a1)
        m2, a2 = _flash_step(q2, k, v, m2, a2)
        return m1, a1, m2, a2

    def near(j, carry):
        m1, a1, m2, a2 = carry
        k, v = kv(j)
        b = bias_ref[0, i - j]
        m1, a1 = _flash_step(q1, k, v, m1, a1, b)
        m2, a2 = _flash_step(q2, k, v, m2, a2, b)
        return m1, a1, m2, a2

    m0 = jnp.full((t, 1), NEG, F32)
    a0 = jnp.zeros((t, LANES), F32)
    carry = lax.fori_loop(0, n_far, far, (m0, a0, m0, a0))
    m1, a1, m2, a2 = lax.fori_loop(n_far, i + 1, near, carry)
    d1 = jnp.broadcast_to(a1[:, HEAD_DIM:HEAD_DIM + 1], (t, LANES))
    d2 = jnp.broadcast_to(a2[:, HEAD_DIM:HEAD_DIM + 1], (t, LANES))
    lane = lax.broadcasted_iota(jnp.int32, (t, LANES), 1)
    o = jnp.where(lane < HEAD_DIM, a1 / d1 - lam_ref[0] * (a2 / d2), 0.0)
    ms = jnp.sum(o * o, axis=1, keepdims=True) * (1.0 / HEAD_DIM)
    o_ref[0, 0] = (o * lax.rsqrt(ms + RMS_EPS) * gain_ref[...] * post).astype(BF16)


def _diff_attention(lam, q1, q2, k, v, bias, gain, t, post):
    B, H, S, _ = q1.shape
    n_near = bias.shape[1]
    qspec = pl.BlockSpec((1, 1, t, LANES), lambda b, h, i: (b, h, i, 0))
    kspec = pl.BlockSpec((1, 1, S, LANES), lambda b, h, i: (b, h, 0, 0))
    return pl.pallas_call(
        functools.partial(_diff_kernel, t=t, n_near=n_near, post=post),
        grid=(B, H, S // t),
        in_specs=[pl.BlockSpec(memory_space=pltpu.SMEM), qspec, qspec, kspec, kspec,
                  pl.BlockSpec((1, n_near, t, t), lambda b, h, i: (h, 0, 0, 0)),
                  pl.BlockSpec((1, LANES), lambda b, h, i: (0, 0))],
        out_specs=qspec,
        out_shape=jax.ShapeDtypeStruct((B, H, S, LANES), BF16),
        compiler_params=_cparams(("arbitrary", "arbitrary", "arbitrary")),
        name="diff_attention",
    )(lam, q1, q2, k, v, bias, gain)


def _dil_kernel(q_ref, kp_ref, kc_ref, vp_ref, vc_ref, bias_ref, num_ref, st_ref):
    blk = pl.program_id(1)
    q = q_ref[0]
    kp, kc, vp, vc = kp_ref[0], kc_ref[0], vp_ref[0], vc_ref[0]
    w = q.shape[0]
    lane = lax.broadcasted_iota(jnp.int32, (w, GROUP), 1)
    lane_s = lax.broadcasted_iota(jnp.int32, (w, LANES), 1)
    prev_ok = jnp.where(blk > 0, 0.0, NEG)
    num = jnp.zeros((w, GROUP), F32)
    st = jnp.zeros((w, LANES), F32)
    for hh in range(N_HEADS):
        in_head = (lane >= hh * HEAD_DIM) & (lane < (hh + 1) * HEAD_DIM)
        qh = jnp.where(in_head, q, jnp.zeros_like(q))
        sp = _dot_nt(qh, kp) + bias_ref[hh, :, 0:DIL_BLOCK] + prev_ok
        sc = _dot_nt(qh, kc) + bias_ref[hh, :, DIL_BLOCK:2 * DIL_BLOCK]
        m = jnp.maximum(jnp.max(sp, axis=1, keepdims=True), jnp.max(sc, axis=1, keepdims=True))
        pp = jnp.exp(sp - m)
        pc = jnp.exp(sc - m)
        den = jnp.sum(pp, axis=1, keepdims=True) + jnp.sum(pc, axis=1, keepdims=True)
        o = _dot(pp.astype(BF16), vp) + _dot(pc.astype(BF16), vc)
        num = jnp.where(in_head, o, num)
        st = jnp.where(lane_s == hh, den, jnp.where(lane_s == N_HEADS + hh, m, st))
    num_ref[0] = num
    st_ref[0] = st


def _dil_branch(q, k, v, bias, dil):
    B, S, _ = q.shape
    L = S // dil
    nb = L // DIL_BLOCK
    qv, kv_, vv = (a.reshape(B, L, dil * GROUP) for a in (q, k, v))
    cur = pl.BlockSpec((1, DIL_BLOCK, GROUP), lambda b, n, r: (b, n, r))
    prev = pl.BlockSpec((1, DIL_BLOCK, GROUP), lambda b, n, r: (b, jnp.maximum(n - 1, 0), r))
    num, st = pl.pallas_call(
        _dil_kernel,
        grid=(B, nb, dil),
        in_specs=[cur, prev, cur, prev, cur,
                  pl.BlockSpec(bias.shape, lambda b, n, r: (0, 0, 0))],
        out_specs=[cur, pl.BlockSpec((1, DIL_BLOCK, LANES), lambda b, n, r: (b, n, r))],
        out_shape=[jax.ShapeDtypeStruct((B, L, dil * GROUP), F32),
                   jax.ShapeDtypeStruct((B, L, dil * LANES), F32)],
        compiler_params=_cparams(("arbitrary", "arbitrary", "arbitrary")),
        name="dilated_branch_%d" % dil,
    )(qv, kv_, kv_, vv, vv, bias)
    return num.reshape(B, S, GROUP), st.reshape(B, S, LANES)


def _dil_combine_kernel(n0_ref, n1_ref, n2_ref, s0_ref, s1_ref, s2_ref, o_ref):
    nums = [r[0] for r in (n0_ref, n1_ref, n2_ref)]
    sts = [r[0] for r in (s0_ref, s1_ref, s2_ref)]
    tm = nums[0].shape[0]
    lane = lax.broadcasted_iota(jnp.int32, (tm, GROUP), 1)
    out = jnp.zeros((tm, GROUP), F32)
    for hh in range(N_HEADS):
        ms = [s[:, N_HEADS + hh:N_HEADS + hh + 1] for s in sts]
        ds = [s[:, hh:hh + 1] for s in sts]
        mx = jnp.maximum(jnp.maximum(ms[0], ms[1]), ms[2])
        ws = [jnp.exp(m - mx) for m in ms]
        den = ws[0] * ds[0] + ws[1] * ds[1] + ws[2] * ds[2]
        o = (ws[0] * nums[0] + ws[1] * nums[1] + ws[2] * nums[2]) / den
        out = jnp.where((lane >= hh * HEAD_DIM) & (lane < (hh + 1) * HEAD_DIM), o, out)
    o_ref[0] = out.astype(BF16)


def _dil_combine(nums, sts, tm):
    B, S, _ = nums[0].shape
    nspec = pl.BlockSpec((1, tm, GROUP), lambda b, i: (b, i, 0))
    sspec = pl.BlockSpec((1, tm, LANES), lambda b, i: (b, i, 0))
    return pl.pallas_call(
        _dil_combine_kernel,
        grid=(B, S // tm),
        in_specs=[nspec] * 3 + [sspec] * 3,
        out_specs=nspec,
        out_shape=jax.ShapeDtypeStruct((B, S, GROUP), BF16),
        compiler_params=_cparams(("arbitrary", "arbitrary")),
        name="dilated_combine",
    )(*nums, *sts)


def _s5_kernel(u_ref, are_ref, aim_ref, bre_ref, bim_ref, cre_ref, cim_ref, d_ref, gw_ref, gb_ref,
               o_ref, xr_ref, xi_ref, bur_ref, bui_ref, sr_ref, si_ref, *, nb, tc):
    @pl.when(pl.program_id(0) == 0)
    def _():
        sr_ref[...] = jnp.zeros_like(sr_ref)
        si_ref[...] = jnp.zeros_like(si_ref)

    u = u_ref[...]
    ub = u.astype(BF16)
    bur_ref[...] = _dot(ub, bre_ref[...])
    bui_ref[...] = _dot(ub, bim_ref[...])
    ar = jnp.broadcast_to(are_ref[...], (nb, S5_WIDE))
    ai = jnp.broadcast_to(aim_ref[...], (nb, S5_WIDE))

    def step(t, carry):
        xr, xi = carry
        rows = pl.ds(pl.multiple_of(t * nb, nb), nb)
        nr = ar * xr - ai * xi + bur_ref[rows, :]
        ni = ar * xi + ai * xr + bui_ref[rows, :]
        xr_ref[rows, :] = nr
        xi_ref[rows, :] = ni
        return nr, ni

    xr, xi = lax.fori_loop(0, tc, step, (sr_ref[...], si_ref[...]))
    sr_ref[...] = xr
    si_ref[...] = xi
    y = (_dot(xr_ref[...].astype(BF16), cre_ref[...]) - _dot(xi_ref[...].astype(BF16), cim_ref[...])
         + d_ref[...] * u)
    g = _gelu(y)
    gate = _dot(g.astype(BF16), gw_ref[...]) + gb_ref[...]
    o_ref[...] = (g * (1.0 / (1.0 + jnp.exp(-gate)))).astype(BF16)


def _s5(u2, are, aim, bre, bim, cre, cim, d, gw, gb, nb, tc):
    rows = u2.shape[0]
    tr = tc * nb

    def full(a):
        return pl.BlockSpec(a.shape, lambda i: (0,) * a.ndim)

    consts = (are, aim, bre, bim, cre, cim, d, gw, gb)
    return pl.pallas_call(
        functools.partial(_s5_kernel, nb=nb, tc=tc),
        grid=(rows // tr,),
        in_specs=[pl.BlockSpec((tr, GROUP), lambda i: (i, 0))] + [full(a) for a in consts],
        out_specs=pl.BlockSpec((tr, GROUP), lambda i: (i, 0)),
        out_shape=jax.ShapeDtypeStruct((rows, GROUP), BF16),
        scratch_shapes=[pltpu.VMEM((tr, S5_WIDE), F32)] * 4 + [pltpu.VMEM((nb, S5_WIDE), F32)] * 2,
        compiler_params=_cparams(("arbitrary",)),
        name="s5_mixer",
    )(u2, *consts)


def _outproj_kernel(x_ref, fox_ref, dil_ref, dif_ref, s5_ref, wf_ref, wd_ref, wc_ref, ws_ref, o_ref):
    acc = x_ref[0] + _dot(dil_ref[0], wd_ref[...]) + _dot(s5_ref[...], ws_ref[...])
    for hh in range(N_HEADS):
        acc = acc + _dot(fox_ref[0, hh], wf_ref[hh]) + _dot(dif_ref[0, hh], wc_ref[hh])
    o_ref[0] = acc


def _outproj(x, fox, dil, dif, s5, wf, wd, wc, ws, tm):
    B, S, D = x.shape
    xspec = pl.BlockSpec((1, tm, D), lambda b, i: (b, i, 0))
    hspec = pl.BlockSpec((1, N_HEADS, tm, LANES), lambda b, i: (b, 0, i, 0))

    def full(a):
        return pl.BlockSpec(a.shape, lambda b, i: (0,) * a.ndim)

    return pl.pallas_call(
        _outproj_kernel,
        grid=(B, S // tm),
        in_specs=[xspec, hspec, pl.BlockSpec((1, tm, GROUP), lambda b, i: (b, i, 0)), hspec,
                  pl.BlockSpec((tm, GROUP), lambda b, i: (i, b)),
                  full(wf), full(wd), full(wc), full(ws)],
        out_specs=xspec,
        out_shape=jax.ShapeDtypeStruct((B, S, D), F32),
        compiler_params=_cparams(("arbitrary", "arbitrary")),
        name="outproj",
    )(x, fox, dil, dif, s5, wf, wd, wc, ws)


def _staircase(n):
    return [(a, b) for a in range(n) for b in range(n) if (a + 1) * (b + 1) <= n]


def _peer_select_kernel(x_ref, g_ref, wq_ref, keys_ref, h_ref, p1_ref, p2_ref, th_ref):
    x = x_ref[...]
    tt = x.shape[0]
    h = (x * lax.rsqrt(jnp.mean(x * x, axis=-1, keepdims=True) + RMS_EPS) * g_ref[...]).astype(BF16)
    h_ref[...] = h
    q = _dot(h, wq_ref[...]).astype(BF16)
    n_top = PEER_TOPK + 1
    pairs = _staircase(n_top)
    for hh in range(PEER_HEADS):
        tops = []
        scores = []
        for half in range(2):
            c = (hh * 2 + half) * N_KEYS
            s = _dot_nt(keys_ref[hh, half], q[:, c:c + N_KEYS])
            scores.append(s)
            vals = []
            for _ in range(n_top):
                mx = jnp.max(s, axis=0, keepdims=True)
                vals.append(mx)
                s = jnp.where(s == mx, NEG, s)
            tops.append(vals)
        cand = jnp.concatenate([tops[0][a] + tops[1][b] for a, b in pairs]
                               + [jnp.full((8 - len(pairs) % 8, tt), NEG, F32)], axis=0)
        best = []
        for _ in range(n_top):
            mx = jnp.max(cand, axis=0, keepdims=True)
            best.append(mx)
            cand = jnp.where(cand == mx, NEG, cand)
        z = jnp.zeros((1, tt), F32)
        for kk in range(PEER_TOPK):
            z = z + jnp.exp(best[kk] - best[0])
        rz = 1.0 / z
        p1_ref[hh] = jnp.exp(scores[0] - tops[0][0]) * rz
        p2_ref[hh] = jnp.exp(scores[1] - tops[1][0])
        cut = 0.5 * (best[PEER_TOPK - 1] + best[PEER_TOPK])
        th_ref[hh] = jnp.exp(cut - best[0]) * rz


def _peer_select(x2, g, wq, keys, tt):
    T, D = x2.shape
    return pl.pallas_call(
        _peer_select_kernel,
        grid=(T // tt,),
        in_specs=[pl.BlockSpec((tt, D), lambda i: (i, 0)),
                  pl.BlockSpec(g.shape, lambda i: (0, 0)),
                  pl.BlockSpec(wq.shape, lambda i: (0, 0)),
                  pl.BlockSpec(keys.shape, lambda i: (0, 0, 0, 0))],
        out_specs=[pl.BlockSpec((tt, D), lambda i: (i, 0)),
                   pl.BlockSpec((PEER_HEADS, N_KEYS, tt), lambda i: (0, 0, i)),
                   pl.BlockSpec((PEER_HEADS, N_KEYS, tt), lambda i: (0, 0, i)),
                   pl.BlockSpec((PEER_HEADS, 1, tt), lambda i: (0, 0, i))],
        out_shape=[jax.ShapeDtypeStruct((T, D), BF16),
                   jax.ShapeDtypeStruct((PEER_HEADS, N_KEYS, T), F32),
                   jax.ShapeDtypeStruct((PEER_HEADS, N_KEYS, T), F32),
                   jax.ShapeDtypeStruct((PEER_HEADS, 1, T), F32)],
        compiler_params=_cparams(("arbitrary",)),
        name="peer_select",
    )(x2, g, wq, keys)


def _peer_dense_kernel(x_ref, h_ref, p1_ref, p2_ref, th_ref, u_ref, vt_ref, o_ref,
                       a_ref, w_ref, acc_ref, *, te, tt):
    j = pl.program_id(1)

    @pl.when(j == 0)
    def _():
        acc_ref[...] = jnp.zeros_like(acc_ref)

    a_ref[...] = _dot_nt(u_ref[...], h_ref[...])
    rows_per_step = te // N_KEYS
    assert rows_per_step == 8
    rows = pl.ds(pl.multiple_of(j * rows_per_step, rows_per_step), rows_per_step)
    for ii in range(rows_per_step):

        def chunk(c, _):
            cols = pl.ds(pl.multiple_of(c * LANES, LANES), LANES)
            g = jnp.zeros((N_KEYS, LANES), F32)
            for hh in range(PEER_HEADS):
                w = p2_ref[hh, :, cols] * p1_ref[hh, rows, cols][ii:ii + 1]
                g = g + jnp.where(w >= th_ref[hh, :, cols], w, 0.0)
            a = a_ref[ii * N_KEYS:(ii + 1) * N_KEYS, cols]
            w_ref[ii * N_KEYS:(ii + 1) * N_KEYS, cols] = (_gelu(a) * g).astype(BF16)
            return 0

        lax.fori_loop(0, tt // LANES, chunk, 0)
    acc_ref[...] += _dot(vt_ref[...], w_ref[...])

    @pl.when(j == pl.num_programs(1) - 1)
    def _():
        o_ref[...] = x_ref[...] + acc_ref[...].T


def _peer_dense(x2, h, p1, p2, th, u, vt, tt, te):
    T, D = x2.shape
    E = u.shape[0]
    return pl.pallas_call(
        functools.partial(_peer_dense_kernel, te=te, tt=tt),
        grid=(T // tt, E // te),
        in_specs=[pl.BlockSpec((tt, D), lambda i, j: (i, 0)),
                  pl.BlockSpec((tt, D), lambda i, j: (i, 0)),
                  pl.BlockSpec((PEER_HEADS, N_KEYS, tt), lambda i, j: (0, 0, i)),
                  pl.BlockSpec((PEER_HEADS, N_KEYS, tt), lambda i, j: (0, 0, i)),
                  pl.BlockSpec((PEER_HEADS, 1, tt), lambda i, j: (0, 0, i)),
                  pl.BlockSpec((te, D), lambda i, j: (j, 0)),
                  pl.BlockSpec((D, te), lambda i, j: (0, j))],
        out_specs=pl.BlockSpec((tt, D), lambda i, j: (i, 0)),
        out_shape=jax.ShapeDtypeStruct((T, D), F32),
        scratch_shapes=[pltpu.VMEM((te, tt), F32), pltpu.VMEM((te, tt), BF16), pltpu.VMEM((D, tt), F32)],
        compiler_params=_cparams(("arbitrary", "arbitrary")),
        name="peer_dense",
    )(x2, h, p1, p2, th, u, vt)


def _block_ones(width):
    idx = np.arange(GROUP) // width
    return jnp.asarray(idx[:, None] == idx[None, :], BF16)


def _pad_head_rows(w):
    w = w.reshape(N_HEADS, HEAD_DIM, -1)
    return jnp.concatenate([w, jnp.zeros_like(w)], axis=1).astype(BF16)


def _dil_bias(rel_bias, window, dil):
    ws = window // dil
    j = jnp.arange(DIL_BLOCK)[:, None]
    m = jnp.arange(2 * DIL_BLOCK)[None, :]
    sub = DIL_BLOCK + j - m
    band = (sub >= 0) & (sub <= ws)
    bias = rel_bias[_t5_bucket(dil * jnp.clip(sub, 0, ws))].transpose(2, 0, 1).astype(F32)
    return jnp.where(band[None], bias, NEG)


def _diff_bias(rel_bias, t, n_near):
    d = jnp.arange(n_near)[:, None, None] * t + jnp.arange(t)[None, :, None] - jnp.arange(t)[None, None, :]
    bias = rel_bias[_t5_bucket(d)].astype(F32) - rel_bias[REL_BUCKETS - 1].astype(F32)
    bias = jnp.where((d >= 0)[..., None], bias, NEG)
    return bias.transpose(3, 0, 1, 2)


def _s5_params(a_re, a_im, log_dt, b_re, b_im, c_re, c_im, d_skip):
    G, N, C = S5_GROUPS, S5_STATE, S5_CH
    dt = jnp.exp(log_dt.astype(F32))[:, None]
    ar, ai = a_re.astype(F32), a_im.astype(F32)
    mag = jnp.exp(dt * ar)
    abar_re, abar_im = mag * jnp.cos(dt * ai), mag * jnp.sin(dt * ai)
    den = ar * ar + ai * ai
    nr, ni = abar_re - 1.0, abar_im
    coef_re = (nr * ar + ni * ai) / den
    coef_im = (ni * ar - nr * ai) / den
    br, bi = b_re.astype(F32), b_im.astype(F32)
    bbar_re = coef_re[..., None] * br - coef_im[..., None] * bi
    bbar_im = coef_re[..., None] * bi + coef_im[..., None] * br
    eye = jnp.eye(G, dtype=F32)

    def b_blockdiag(bb):
        return jnp.einsum('gnc,gk->gckn', bb, eye).reshape(G * C, G * N).astype(BF16)

    def c_blockdiag(cc):
        return jnp.einsum('gcn,gk->gnkc', cc.astype(F32), eye).reshape(G * N, G * C).astype(BF16)

    return (abar_re.reshape(1, G * N), abar_im.reshape(1, G * N), b_blockdiag(bbar_re), b_blockdiag(bbar_im),
            c_blockdiag(c_re), c_blockdiag(c_im), d_skip.astype(F32).reshape(1, G * C))


def _pick(n, pref):
    for c in pref:
        if n % c == 0:
            return c
    raise ValueError("unsupported size %d" % n)


def kernel(x, w_in, b_forget, w_out, norm_mix, norm_ffn, fox_q_norm, fox_k_norm, dil_q_norm, dil_k_norm, diff_q_norm, diff_k_norm, diff_lambda, diff_out_norm, rel_bias, s5_a_re, s5_a_im, s5_log_dt, s5_b_re, s5_b_im, s5_c_re, s5_c_im, s5_d, s5_glu_w, s5_glu_b, peer_wq, peer_keys, peer_u, peer_v):
    B, S, D = x.shape
    depth = w_in.shape[0]
    T = B * S
    assert S % (DIL_BLOCK * DIL_PAIRS[-1][1]) == 0 and D % LANES == 0
    tm = _pick(S, (512, 256, 128))
    t_att = 256
    tk_fox = _pick(S, (512, 256))
    tc = _pick(S, (128, 64))
    tt_sel = _pick(T, (512, 256))
    tt_peer = _pick(T, (512, 256))
    te = 1024

    n_sat = _bucket_saturation()
    n_near = min(-(-(n_sat + t_att - 1) // t_att), S // t_att)
    rel_bias = rel_bias.astype(F32)
    dil_bias = [_dil_bias(rel_bias[:, :N_HEADS], w, r) for w, r in DIL_PAIRS]
    diff_bias = _diff_bias(rel_bias[:, N_HEADS:], t_att, n_near)
    far = rel_bias[REL_BUCKETS - 1, N_HEADS:]
    far_hi = far.astype(BF16).astype(F32)
    lane = jnp.arange(LANES)
    aug = jnp.zeros((2 + N_HEADS, LANES), F32)
    aug = aug.at[0].set(jnp.where(lane == HEAD_DIM, 1.0, 0.0))
    aug = aug.at[1].set(jnp.where((lane == HEAD_DIM) | (lane == HEAD_DIM + 1), 1.0, 0.0))
    aug = aug.at[2:].set(jnp.where(lane[None] == HEAD_DIM, far_hi[:, None],
                                   jnp.where(lane[None] == HEAD_DIM + 1, (far - far_hi)[:, None], 0.0)))
    ones64, ones32 = _block_ones(HEAD_DIM), _block_ones(DIFF_HALF)
    tri = jnp.asarray(np.tril(np.ones((tm, tm), np.float32)), BF16)

    sizes = [GROUP, GROUP, GROUP, N_HEADS] + [GROUP] * 7
    starts = np.cumsum([0] + sizes)
    order = [0, 1, 2, 4, 5, 6, 7, 8, 9, 10]

    x = x.astype(F32)
    for layer in range(depth):
        wl = w_in[layer]
        w_packed = jnp.concatenate(
            [wl[:, starts[k]:starts[k + 1]] for k in order]
            + [wl[:, starts[3]:starts[4]], jnp.zeros((D, LANES - N_HEADS), wl.dtype)], axis=1).astype(BF16)
        bf = jnp.zeros((1, LANES), F32).at[0, :N_HEADS].set(b_forget[layer].astype(F32))
        gains = jnp.stack([
            jnp.tile(fox_q_norm[layer].astype(F32), N_HEADS) * HEAD_DIM ** -0.5,
            jnp.tile(fox_k_norm[layer].astype(F32), N_HEADS),
            jnp.tile(dil_q_norm[layer].astype(F32), N_HEADS) * HEAD_DIM ** -0.5,
            jnp.tile(dil_k_norm[layer].astype(F32), N_HEADS),
            jnp.tile(diff_q_norm[layer].astype(F32), 2 * N_HEADS) * DIFF_HALF ** -0.5,
            jnp.tile(diff_k_norm[layer].astype(F32), 2 * N_HEADS),
            jnp.zeros((GROUP,), F32), jnp.zeros((GROUP,), F32)])
        (fq, fk, fv, cq1, cq2, ck, cv, dq, dk, dv, su) = _inproj(
            x, norm_mix[layer].astype(F32)[None], w_packed, bf, gains, ones64, ones32, tri, aug, tm)

        fox_o = _fox_attention(fq, fk, fv, t_att, tk_fox)

        lam_init = 0.8 - 0.6 * math.exp(-0.3 * layer)
        lam_p = diff_lambda[layer].astype(F32)
        lam = jnp.exp(jnp.sum(lam_p[0] * lam_p[1])) - jnp.exp(jnp.sum(lam_p[2] * lam_p[3])) + lam_init
        out_gain = jnp.zeros((1, LANES), F32).at[0, :HEAD_DIM].set(diff_out_norm[layer].astype(F32))
        diff_o = _diff_attention(lam.reshape(1), cq1, cq2, ck, cv, diff_bias, out_gain, t_att, 1.0 - lam_init)

        branches = [_dil_branch(dq, dk, dv, dil_bias[n], r) for n, (_, r) in enumerate(DIL_PAIRS)]
        dil_o = _dil_combine([b[0] for b in branches], [b[1] for b in branches], tm)

        s5p = _s5_params(s5_a_re[layer], s5_a_im[layer], s5_log_dt[layer], s5_b_re[layer], s5_b_im[layer],
                         s5_c_re[layer], s5_c_im[layer], s5_d[layer])
        s5_o = _s5(su.reshape(S * B, GROUP), *s5p, s5_glu_w[layer].astype(BF16),
                   s5_glu_b[layer].astype(F32)[None], B, tc).reshape(S, B * GROUP)

        wo = w_out[layer]
        x = _outproj(x, fox_o, dil_o, diff_o, s5_o,
                     _pad_head_rows(wo[0:GROUP]), wo[GROUP:2 * GROUP].astype(BF16),
                     _pad_head_rows(wo[2 * GROUP:3 * GROUP]), wo[3 * GROUP:].astype(BF16), tm)

        x2 = x.reshape(T, D)
        h, p1, p2, th = _peer_select(x2, norm_ffn[layer].astype(F32)[None], peer_wq[layer].astype(BF16),
                                     peer_keys[layer].astype(BF16), tt_sel)
        x = _peer_dense(x2, h, p1, p2, th, peer_u[layer].astype(BF16), peer_v[layer].astype(BF16).T,
                        tt_peer, te).reshape(B, S, D)
    return x
```

```python
import functools
import math

import numpy as np
import jax
import jax.numpy as jnp
from jax import lax
from jax.experimental import pallas as pl
from jax.experimental.pallas import tpu as pltpu

F32 = jnp.float32
BF16 = jnp.bfloat16

LANES = 128
GROUP = 256
HEAD_DIM = 64
N_HEADS = GROUP // HEAD_DIM
DIFF_HALF = HEAD_DIM // 2
S5_CH = 16
S5_GROUPS = GROUP // S5_CH
S5_STATE = 64
S5_WIDE = S5_GROUPS * S5_STATE
DIL_BLOCK = 128
DIL_PAIRS = ((128, 1), (512, 4), (2048, 16))
REL_BUCKETS = 32
REL_MAX_EXACT = REL_BUCKETS // 2
REL_MAX_DIST = 2048
N_KEYS = 128
PEER_HEADS = 8
PEER_TOPK = 16
RMS_EPS = 1e-6
NEG = -1e30
LOG2E = math.log2(math.e)
VMEM_LIMIT = 56 * 1024 * 1024

IN_COLS = 10 * GROUP + LANES
FORGET_COL = 10 * GROUP


def _cparams(sem):
    return pltpu.CompilerParams(dimension_semantics=sem, vmem_limit_bytes=VMEM_LIMIT)


def _dot(a, b):
    return jnp.dot(a, b, preferred_element_type=F32)


def _dot_nt(a, b):
    return lax.dot_general(a, b, (((1,), (1,)), ((), ())), preferred_element_type=F32)


def _split3(x):
    hi = x.astype(BF16)
    r = x - hi.astype(F32)
    mid = r.astype(BF16)
    lo = (r - mid.astype(F32)).astype(BF16)
    return hi, mid, lo


def _gelu(x):
    return 0.5 * x * (1.0 + jnp.tanh(math.sqrt(2.0 / math.pi) * (x + 0.044715 * (x * x * x))))


def _t5_bucket(dist):
    n = jnp.maximum(dist, 0)
    nf = jnp.maximum(n, REL_MAX_EXACT).astype(F32)
    large = REL_MAX_EXACT + (jnp.log(nf / REL_MAX_EXACT) / math.log(REL_MAX_DIST / REL_MAX_EXACT)
                             * (REL_BUCKETS - REL_MAX_EXACT)).astype(jnp.int32)
    large = jnp.minimum(large, REL_BUCKETS - 1)
    return jnp.where(n < REL_MAX_EXACT, n, large)


def _bucket_saturation():
    n = REL_MAX_EXACT * (REL_MAX_DIST / REL_MAX_EXACT) ** ((REL_BUCKETS - 1 - REL_MAX_EXACT)
                                                            / (REL_BUCKETS - REL_MAX_EXACT))
    return int(math.ceil(n)) + 2


def _inproj_kernel(x_ref, g_ref, w_ref, bf_ref, gains_ref, ones64_ref, ones32_ref, tri_ref, aug_ref,
                   fq_ref, fk_ref, fv_ref, cq1_ref, cq2_ref, ck_ref, cv_ref,
                   dq_ref, dk_ref, dv_ref, su_ref, carry_ref):
    @pl.when(pl.program_id(1) == 0)
    def _():
        carry_ref[...] = jnp.zeros_like(carry_ref)

    x = x_ref[0]
    tm = x.shape[0]
    h = x * lax.rsqrt(jnp.mean(x * x, axis=-1, keepdims=True) + RMS_EPS) * g_ref[...]
    z = _dot(h.astype(BF16), w_ref[...])

    def group(idx):
        return z[:, idx * GROUP:(idx + 1) * GROUP]

    def head_norm(zg, ones_ref, width, gain_row):
        ss = _dot((zg * zg).astype(BF16), ones_ref[...])
        return zg * lax.rsqrt(ss * (1.0 / width) + RMS_EPS) * gains_ref[gain_row:gain_row + 1, :]

    lane = lax.broadcasted_iota(jnp.int32, (tm, LANES), 1)

    def head_chunk(zg, hh):
        c = zg[:, (hh // 2) * LANES:(hh // 2 + 1) * LANES]
        if hh % 2:
            c = pltpu.roll(c, HEAD_DIM, axis=1)
        return c

    f = z[:, FORGET_COL:FORGET_COL + LANES] + bf_ref[...]
    ls = jnp.minimum(f, 0.0) - jnp.log1p(jnp.exp(-jnp.abs(f)))
    p0, p1, p2 = _split3(ls)
    tri = tri_ref[...]
    cum = (_dot(tri, p0) + _dot(tri, p1)) + _dot(tri, p2) + carry_ref[...]
    carry_ref[...] = cum[tm - 1:tm, :]

    fq = head_norm(group(0), ones64_ref, HEAD_DIM, 0)
    fk = head_norm(group(1), ones64_ref, HEAD_DIM, 1)
    fv = group(2)
    dq = head_norm(group(3), ones64_ref, HEAD_DIM, 2)
    dk = head_norm(group(4), ones64_ref, HEAD_DIM, 3)
    cq = head_norm(group(6), ones32_ref, DIFF_HALF, 4)
    ck = head_norm(group(7), ones32_ref, DIFF_HALF, 5)
    cv = group(8)

    dq_ref[0] = dq.astype(BF16)
    dk_ref[0] = dk.astype(BF16)
    dv_ref[0] = group(5).astype(BF16)
    su_ref[...] = group(9)

    one = jnp.ones((tm, LANES), F32)
    zero = jnp.zeros((tm, LANES), F32)
    for hh in range(N_HEADS):
        c = jnp.broadcast_to(cum[:, hh:hh + 1], (tm, LANES)) * LOG2E
        c0, c1, c2 = _split3(c)
        c0, c1, c2 = c0.astype(F32), c1.astype(F32), c2.astype(F32)
        qa = jnp.where(lane == 64, c0, jnp.where(lane == 65, c1, jnp.where(lane == 66, c2,
             jnp.where(lane < 70, one, zero))))
        ka = jnp.where(lane == 67, -c0, jnp.where(lane == 68, -c1, jnp.where(lane == 69, -c2,
             jnp.where(lane < 70, one, zero))))
        fq_ref[0, hh] = jnp.where(lane < HEAD_DIM, head_chunk(fq, hh), qa).astype(BF16)
        fk_ref[0, hh] = jnp.where(lane < HEAD_DIM, head_chunk(fk, hh), ka).astype(BF16)
        vaug = aug_ref[0:1, :]
        fv_ref[0, hh] = jnp.where(lane < HEAD_DIM, head_chunk(fv, hh), vaug).astype(BF16)
        qaug = aug_ref[1:2, :]
        cqc = head_chunk(cq, hh)
        cq1_ref[0, hh] = jnp.where(lane < DIFF_HALF, cqc, qaug).astype(BF16)
        cq2_ref[0, hh] = jnp.where((lane >= DIFF_HALF) & (lane < HEAD_DIM), cqc, qaug).astype(BF16)
        kaug = aug_ref[2 + hh:3 + hh, :]
        ck_ref[0, hh] = jnp.where(lane < HEAD_DIM, head_chunk(ck, hh), kaug).astype(BF16)
        cv_ref[0, hh] = jnp.where(lane < HEAD_DIM, head_chunk(cv, hh), vaug).astype(BF16)


def _inproj(x, g, w, bf, gains, ones64, ones32, tri, aug, tm):
    B, S, D = x.shape
    nt = S // tm
    hm = jax.ShapeDtypeStruct((B, N_HEADS, S, LANES), BF16)
    nat = jax.ShapeDtypeStruct((B, S, GROUP), BF16)
    hm_spec = pl.BlockSpec((1, N_HEADS, tm, LANES), lambda b, i: (b, 0, i, 0))
    nat_spec = pl.BlockSpec((1, tm, GROUP), lambda b, i: (b, i, 0))

    def full(a):
        return pl.BlockSpec(a.shape, lambda b, i: (0,) * a.ndim)

    return pl.pallas_call(
        _inproj_kernel,
        grid=(B, nt),
        in_specs=[pl.BlockSpec((1, tm, D), lambda b, i: (b, i, 0)),
                  full(g), full(w), full(bf), full(gains), full(ones64), full(ones32), full(tri), full(aug)],
        out_specs=[hm_spec] * 7 + [nat_spec] * 3 + [pl.BlockSpec((tm, GROUP), lambda b, i: (i, b))],
        out_shape=[hm] * 7 + [nat] * 3 + [jax.ShapeDtypeStruct((S, B * GROUP), F32)],
        scratch_shapes=[pltpu.VMEM((1, LANES), F32)],
        compiler_params=_cparams(("arbitrary", "arbitrary")),
        name="inproj",
    )(x, g, w, bf, gains, ones64, ones32, tri, aug)


def _flash_step(q, k, v, m, acc, bias=None):
    s = _dot_nt(q, k)
    if bias is not None:
        s = s + bias
    m_new = jnp.maximum(m, jnp.max(s, axis=1, keepdims=True))
    alpha = jnp.exp2(m - m_new)
    p = jnp.exp2(s - m_new)
    acc = alpha * acc + _dot(p.astype(BF16), v)
    return m_new, acc


def _fox_kernel(q_ref, k_ref, v_ref, o_ref, *, tq, tk, tkd):
    i = pl.program_id(2)
    q = q_ref[0, 0]

    def kv(off, size):
        return k_ref[0, 0, pl.ds(off, size), :], v_ref[0, 0, pl.ds(off, size), :]

    def body(j, carry):
        m, acc = carry
        k, v = kv(pl.multiple_of(j * tk, tk), tk)
        return _flash_step(q, k, v, m, acc)

    m0 = jnp.full((tq, 1), NEG, F32)
    acc0 = jnp.zeros((tq, LANES), F32)
    m, acc = lax.fori_loop(0, i * (tq // tk), body, (m0, acc0))
    for d in range(tq // tkd):
        rows = tq - d * tkd
        k, v = kv(pl.multiple_of(i * tq + d * tkd, tkd), tkd)
        row = lax.broadcasted_iota(jnp.int32, (rows, tkd), 0)
        col = lax.broadcasted_iota(jnp.int32, (rows, tkd), 1)
        mask = jnp.where(col <= row, 0.0, NEG)
        m_d, acc_d = _flash_step(q[d * tkd:], k, v, m[d * tkd:], acc[d * tkd:], mask)
        if d:
            m = jnp.concatenate([m[:d * tkd], m_d], axis=0)
            acc = jnp.concatenate([acc[:d * tkd], acc_d], axis=0)
        else:
            m, acc = m_d, acc_d
    den = jnp.broadcast_to(acc[:, HEAD_DIM:HEAD_DIM + 1], (tq, LANES))
    o_ref[0, 0] = (acc / den).astype(BF16)


def _fox_attention(q, k, v, tq, tk, tkd):
    B, H, S, _ = q.shape
    return pl.pallas_call(
        functools.partial(_fox_kernel, tq=tq, tk=tk, tkd=tkd),
        grid=(B, H, S // tq),
        in_specs=[pl.BlockSpec((1, 1, tq, LANES), lambda b, h, i: (b, h, i, 0)),
                  pl.BlockSpec((1, 1, S, LANES), lambda b, h, i: (b, h, 0, 0)),
                  pl.BlockSpec((1, 1, S, LANES), lambda b, h, i: (b, h, 0, 0))],
        out_specs=pl.BlockSpec((1, 1, tq, LANES), lambda b, h, i: (b, h, i, 0)),
        out_shape=jax.ShapeDtypeStruct((B, H, S, LANES), BF16),
        compiler_params=_cparams(("arbitrary", "arbitrary", "arbitrary")),
        name="fox_attention",
    )(q, k, v)


def _diff_kernel(lam_ref, q1_ref, q2_ref, k_ref, v_ref, bias_ref, gain_ref, o_ref, *, t, n_near):
    i = pl.program_id(2)
    q = jnp.concatenate([q1_ref[0, 0], q2_ref[0, 0]], axis=0)
    n_far = jnp.maximum(i - (n_near - 1), 0)

    def kv(j):
        off = pl.multiple_of(j * t, t)
        return k_ref[0, 0, pl.ds(off, t), :], v_ref[0, 0, pl.ds(off, t), :]

    def far(j, carry):
        m, acc = carry
        k, v = kv(j)
        return _flash_step(q, k, v, m, acc)

    def near(j, carry):
        m, acc = carry
        k, v = kv(j)
        b = bias_ref[0, i - j]
        return _flash_step(q, k, v, m, acc, jnp.concatenate([b, b], axis=0))

    m0 = jnp.full((2 * t, 1), NEG, F32)
    a0 = jnp.zeros((2 * t, LANES), F32)
    carry = lax.fori_loop(0, n_far, far, (m0, a0))
    _, acc = lax.fori_loop(n_far, i + 1, near, carry)
    a1, a2 = acc[:t], acc[t:]
    d1 = jnp.broadcast_to(a1[:, HEAD_DIM:HEAD_DIM + 1], (t, LANES))
    d2 = jnp.broadcast_to(a2[:, HEAD_DIM:HEAD_DIM + 1], (t, LANES))
    lane = lax.broadcasted_iota(jnp.int32, (t, LANES), 1)
    o = jnp.where(lane < HEAD_DIM, a1 / d1 - lam_ref[0] * (a2 / d2), 0.0)
    ms = jnp.sum(o * o, axis=1, keepdims=True) * (1.0 / HEAD_DIM)
    o_ref[0, 0] = (o * lax.rsqrt(ms + RMS_EPS) * gain_ref[...] * lam_ref[1]).astype(BF16)


def _diff_attention(lam, q1, q2, k, v, bias, gain, t):
    B, H, S, _ = q1.shape
    n_near = bias.shape[1]
    qspec = pl.BlockSpec((1, 1, t, LANES), lambda b, h, i: (b, h, i, 0))
    kspec = pl.BlockSpec((1, 1, S, LANES), lambda b, h, i: (b, h, 0, 0))
    return pl.pallas_call(
        functools.partial(_diff_kernel, t=t, n_near=n_near),
        grid=(B, H, S // t),
        in_specs=[pl.BlockSpec(memory_space=pltpu.SMEM), qspec, qspec, kspec, kspec,
                  pl.BlockSpec((1, n_near, t, t), lambda b, h, i: (h, 0, 0, 0)),
                  pl.BlockSpec((1, LANES), lambda b, h, i: (0, 0))],
        out_specs=qspec,
        out_shape=jax.ShapeDtypeStruct((B, H, S, LANES), BF16),
        compiler_params=_cparams(("arbitrary", "arbitrary", "arbitrary")),
        name="diff_attention",
    )(lam, q1, q2, k, v, bias, gain)


def _dil_kernel(q_ref, kp_ref, kc_ref, vp_ref, vc_ref, bias_ref, num_ref, st_ref):
    blk = pl.program_id(1)
    q = q_ref[0]
    kp, kc, vp, vc = kp_ref[0], kc_ref[0], vp_ref[0], vc_ref[0]
    w = q.shape[0]
    lane = lax.broadcasted_iota(jnp.int32, (w, GROUP), 1)
    lane_s = lax.broadcasted_iota(jnp.int32, (w, LANES), 1)
    prev_ok = jnp.where(blk > 0, 0.0, NEG)
    num = jnp.zeros((w, GROUP), F32)
    st = jnp.zeros((w, LANES), F32)
    for hh in range(N_HEADS):
        in_head = (lane >= hh * HEAD_DIM) & (lane < (hh + 1) * HEAD_DIM)
        qh = jnp.where(in_head, q, jnp.zeros_like(q))
        sp = _dot_nt(qh, kp) + bias_ref[hh, :, 0:DIL_BLOCK] + prev_ok
        sc = _dot_nt(qh, kc) + bias_ref[hh, :, DIL_BLOCK:2 * DIL_BLOCK]
        m = jnp.maximum(jnp.max(sp, axis=1, keepdims=True), jnp.max(sc, axis=1, keepdims=True))
        pp = jnp.exp(sp - m)
        pc = jnp.exp(sc - m)
        den = jnp.sum(pp, axis=1, keepdims=True) + jnp.sum(pc, axis=1, keepdims=True)
        o = _dot(pp.astype(BF16), vp) + _dot(pc.astype(BF16), vc)
        num = jnp.where(in_head, o, num)
        st = jnp.where(lane_s == hh, den, jnp.where(lane_s == N_HEADS + hh, m, st))
    num_ref[0] = num
    st_ref[0] = st


def _dil_branch(q, k, v, bias, dil):
    B, S, _ = q.shape
    L = S // dil
    nb = L // DIL_BLOCK
    qv, kv_, vv = (a.reshape(B, L, dil * GROUP) for a in (q, k, v))
    cur = pl.BlockSpec((1, DIL_BLOCK, GROUP), lambda b, n, r: (b, n, r))
    prev = pl.BlockSpec((1, DIL_BLOCK, GROUP), lambda b, n, r: (b, jnp.maximum(n - 1, 0), r))
    num, st = pl.pallas_call(
        _dil_kernel,
        grid=(B, nb, dil),
        in_specs=[cur, prev, cur, prev, cur,
                  pl.BlockSpec(bias.shape, lambda b, n, r: (0, 0, 0))],
        out_specs=[cur, pl.BlockSpec((1, DIL_BLOCK, LANES), lambda b, n, r: (b, n, r))],
        out_shape=[jax.ShapeDtypeStruct((B, L, dil * GROUP), F32),
                   jax.ShapeDtypeStruct((B, L, dil * LANES), F32)],
        compiler_params=_cparams(("arbitrary", "arbitrary", "arbitrary")),
        name="dilated_branch_%d" % dil,
    )(qv, kv_, kv_, vv, vv, bias)
    return num.reshape(B, S, GROUP), st.reshape(B, S, LANES)


def _dil_combine_kernel(n0_ref, n1_ref, n2_ref, s0_ref, s1_ref, s2_ref, o_ref):
    nums = [r[0] for r in (n0_ref, n1_ref, n2_ref)]
    sts = [r[0] for r in (s0_ref, s1_ref, s2_ref)]
    tm = nums[0].shape[0]
    lane = lax.broadcasted_iota(jnp.int32, (tm, GROUP), 1)
    out = jnp.zeros((tm, GROUP), F32)
    for hh in range(N_HEADS):
        ms = [s[:, N_HEADS + hh:N_HEADS + hh + 1] for s in sts]
        ds = [s[:, hh:hh + 1] for s in sts]
        mx = jnp.maximum(jnp.maximum(ms[0], ms[1]), ms[2])
        ws = [jnp.exp(m - mx) for m in ms]
        den = ws[0] * ds[0] + ws[1] * ds[1] + ws[2] * ds[2]
        o = (ws[0] * nums[0] + ws[1] * nums[1] + ws[2] * nums[2]) / den
        out = jnp.where((lane >= hh * HEAD_DIM) & (lane < (hh + 1) * HEAD_DIM), o, out)
    o_ref[0] = out.astype(BF16)


def _dil_combine(nums, sts, tm):
    B, S, _ = nums[0].shape
    nspec = pl.BlockSpec((1, tm, GROUP), lambda b, i: (b, i, 0))
    sspec = pl.BlockSpec((1, tm, LANES), lambda b, i: (b, i, 0))
    return pl.pallas_call(
        _dil_combine_kernel,
        grid=(B, S // tm),
        in_specs=[nspec] * 3 + [sspec] * 3,
        out_specs=nspec,
        out_shape=jax.ShapeDtypeStruct((B, S, GROUP), BF16),
        compiler_params=_cparams(("arbitrary", "arbitrary")),
        name="dilated_combine",
    )(*nums, *sts)


def _s5_kernel(u_ref, are_ref, aim_ref, bre_ref, bim_ref, cre_ref, cim_ref, d_ref, gw_ref, gb_ref,
               o_ref, xr_ref, xi_ref, bur_ref, bui_ref, sr_ref, si_ref, *, nb, tc):
    @pl.when(pl.program_id(0) == 0)
    def _():
        sr_ref[...] = jnp.zeros_like(sr_ref)
        si_ref[...] = jnp.zeros_like(si_ref)

    u = u_ref[...]
    ub = u.astype(BF16)
    bur_ref[...] = _dot(ub, bre_ref[...])
    bui_ref[...] = _dot(ub, bim_ref[...])
    ar = jnp.broadcast_to(are_ref[...], (nb, S5_WIDE))
    ai = jnp.broadcast_to(aim_ref[...], (nb, S5_WIDE))

    def step(t, carry):
        xr, xi = carry
        rows = pl.ds(pl.multiple_of(t * nb, nb), nb)
        nr = ar * xr - ai * xi + bur_ref[rows, :]
        ni = ar * xi + ai * xr + bui_ref[rows, :]
        xr_ref[rows, :] = nr
        xi_ref[rows, :] = ni
        return nr, ni

    xr, xi = lax.fori_loop(0, tc, step, (sr_ref[...], si_ref[...]))
    sr_ref[...] = xr
    si_ref[...] = xi
    y = (_dot(xr_ref[...].astype(BF16), cre_ref[...]) - _dot(xi_ref[...].astype(BF16), cim_ref[...])
         + d_ref[...] * u)
    g = _gelu(y)
    gate = _dot(g.astype(BF16), gw_ref[...]) + gb_ref[...]
    o_ref[...] = (g * (1.0 / (1.0 + jnp.exp(-gate)))).astype(BF16)


def _s5(u2, are, aim, bre, bim, cre, cim, d, gw, gb, nb, tc):
    rows = u2.shape[0]
    tr = tc * nb

    def full(a):
        return pl.BlockSpec(a.shape, lambda i: (0,) * a.ndim)

    consts = (are, aim, bre, bim, cre, cim, d, gw, gb)
    return pl.pallas_call(
        functools.partial(_s5_kernel, nb=nb, tc=tc),
        grid=(rows // tr,),
        in_specs=[pl.BlockSpec((tr, GROUP), lambda i: (i, 0))] + [full(a) for a in consts],
        out_specs=pl.BlockSpec((tr, GROUP), lambda i: (i, 0)),
        out_shape=jax.ShapeDtypeStruct((rows, GROUP), BF16),
        scratch_shapes=[pltpu.VMEM((tr, S5_WIDE), F32)] * 4 + [pltpu.VMEM((nb, S5_WIDE), F32)] * 2,
        compiler_params=_cparams(("arbitrary",)),
        name="s5_mixer",
    )(u2, *consts)


def _outproj_kernel(x_ref, fox_ref, dil_ref, dif_ref, s5_ref, wf_ref, wd_ref, wc_ref, ws_ref, o_ref):
    acc = x_ref[0] + _dot(dil_ref[0], wd_ref[...]) + _dot(s5_ref[...], ws_ref[...])
    for hh in range(N_HEADS):
        acc = acc + _dot(fox_ref[0, hh], wf_ref[hh]) + _dot(dif_ref[0, hh], wc_ref[hh])
    o_ref[0] = acc


def _outproj(x, fox, dil, dif, s5, wf, wd, wc, ws, tm):
    B, S, D = x.shape
    xspec = pl.BlockSpec((1, tm, D), lambda b, i: (b, i, 0))
    hspec = pl.BlockSpec((1, N_HEADS, tm, LANES), lambda b, i: (b, 0, i, 0))

    def full(a):
        return pl.BlockSpec(a.shape, lambda b, i: (0,) * a.ndim)

    return pl.pallas_call(
        _outproj_kernel,
        grid=(B, S // tm),
        in_specs=[xspec, hspec, pl.BlockSpec((1, tm, GROUP), lambda b, i: (b, i, 0)), hspec,
                  pl.BlockSpec((tm, GROUP), lambda b, i: (i, b)),
                  full(wf), full(wd), full(wc), full(ws)],
        out_specs=xspec,
        out_shape=jax.ShapeDtypeStruct((B, S, D), F32),
        compiler_params=_cparams(("arbitrary", "arbitrary")),
        name="outproj",
    )(x, fox, dil, dif, s5, wf, wd, wc, ws)


def _staircase(n):
    return [(a, b) for a in range(n) for b in range(n) if (a + 1) * (b + 1) <= n]


def _peer_select_kernel(x_ref, g_ref, wq_ref, keys_ref, h_ref, p1_ref, p2_ref, th_ref):
    x = x_ref[...]
    tt = x.shape[0]
    h = (x * lax.rsqrt(jnp.mean(x * x, axis=-1, keepdims=True) + RMS_EPS) * g_ref[...]).astype(BF16)
    h_ref[...] = h
    q = _dot(h, wq_ref[...]).astype(BF16)
    n_top = PEER_TOPK + 1
    pairs = _staircase(n_top)
    for hh in range(PEER_HEADS):
        tops = []
        scores = []
        for half in range(2):
            c = (hh * 2 + half) * N_KEYS
            s = _dot_nt(keys_ref[hh, half], q[:, c:c + N_KEYS])
            scores.append(s)
            vals = []
            for _ in range(n_top):
                mx = jnp.max(s, axis=0, keepdims=True)
                vals.append(mx)
                s = jnp.where(s == mx, NEG, s)
            tops.append(vals)
        cand = jnp.concatenate([tops[0][a] + tops[1][b] for a, b in pairs]
                               + [jnp.full((8 - len(pairs) % 8, tt), NEG, F32)], axis=0)
        best = []
        for _ in range(n_top):
            mx = jnp.max(cand, axis=0, keepdims=True)
            best.append(mx)
            cand = jnp.where(cand == mx, NEG, cand)
        z = jnp.zeros((1, tt), F32)
        for kk in range(PEER_TOPK):
            z = z + jnp.exp(best[kk] - best[0])
        rz = 1.0 / z
        p1_ref[hh] = jnp.exp(scores[0] - tops[0][0]) * rz
        p2_ref[hh] = jnp.exp(scores[1] - tops[1][0])
        cut = 0.5 * (best[PEER_TOPK - 1] + best[PEER_TOPK])
        th_ref[hh] = jnp.exp(cut - best[0]) * rz


def _peer_select(x2, g, wq, keys, tt):
    T, D = x2.shape
    return pl.pallas_call(
        _peer_select_kernel,
        grid=(T // tt,),
        in_specs=[pl.BlockSpec((tt, D), lambda i: (i, 0)),
                  pl.BlockSpec(g.shape, lambda i: (0, 0)),
                  pl.BlockSpec(wq.shape, lambda i: (0, 0)),
                  pl.BlockSpec(keys.shape, lambda i: (0, 0, 0, 0))],
        out_specs=[pl.BlockSpec((tt, D), lambda i: (i, 0)),
                   pl.BlockSpec((PEER_HEADS, N_KEYS, tt), lambda i: (0, 0, i)),
                   pl.BlockSpec((PEER_HEADS, N_KEYS, tt), lambda i: (0, 0, i)),
                   pl.BlockSpec((PEER_HEADS, 1, tt), lambda i: (0, 0, i))],
        out_shape=[jax.ShapeDtypeStruct((T, D), BF16),
                   jax.ShapeDtypeStruct((PEER_HEADS, N_KEYS, T), F32),
                   jax.ShapeDtypeStruct((PEER_HEADS, N_KEYS, T), F32),
                   jax.ShapeDtypeStruct((PEER_HEADS, 1, T), F32)],
        compiler_params=_cparams(("arbitrary",)),
        name="peer_select",
    )(x2, g, wq, keys)


def _peer_dense_kernel(x_ref, h_ref, p1_ref, p2_ref, th_ref, u_ref, vt_ref, o_ref,
                       a_ref, w_ref, acc_ref, *, te, tt):
    j = pl.program_id(1)

    @pl.when(j == 0)
    def _():
        acc_ref[...] = jnp.zeros_like(acc_ref)

    a_ref[...] = _dot_nt(u_ref[...], h_ref[...])
    rows_per_step = te // N_KEYS
    assert rows_per_step == 8
    rows = pl.ds(pl.multiple_of(j * rows_per_step, rows_per_step), rows_per_step)
    for ii in range(rows_per_step):

        def chunk(c, _):
            cols = pl.ds(pl.multiple_of(c * LANES, LANES), LANES)
            g = jnp.zeros((N_KEYS, LANES), F32)
            for hh in range(PEER_HEADS):
                w = p2_ref[hh, :, cols] * p1_ref[hh, rows, cols][ii:ii + 1]
                g = g + jnp.where(w >= th_ref[hh, :, cols], w, 0.0)
            a = a_ref[ii * N_KEYS:(ii + 1) * N_KEYS, cols]
            w_ref[ii * N_KEYS:(ii + 1) * N_KEYS, cols] = (_gelu(a) * g).astype(BF16)
            return 0

        lax.fori_loop(0, tt // LANES, chunk, 0)
    acc_ref[...] += _dot(vt_ref[...], w_ref[...])

    @pl.when(j == pl.num_programs(1) - 1)
    def _():
        o_ref[...] = x_ref[...] + acc_ref[...].T


def _peer_dense(x2, h, p1, p2, th, u, vt, tt, te):
    T, D = x2.shape
    E = u.shape[0]
    return pl.pallas_call(
        functools.partial(_peer_dense_kernel, te=te, tt=tt),
        grid=(T // tt, E // te),
        in_specs=[pl.BlockSpec((tt, D), lambda i, j: (i, 0)),
                  pl.BlockSpec((tt, D), lambda i, j: (i, 0)),
                  pl.BlockSpec((PEER_HEADS, N_KEYS, tt), lambda i, j: (0, 0, i)),
                  pl.BlockSpec((PEER_HEADS, N_KEYS, tt), lambda i, j: (0, 0, i)),
                  pl.BlockSpec((PEER_HEADS, 1, tt), lambda i, j: (0, 0, i)),
                  pl.BlockSpec((te, D), lambda i, j: (j, 0)),
                  pl.BlockSpec((D, te), lambda i, j: (0, j))],
        out_specs=pl.BlockSpec((tt, D), lambda i, j: (i, 0)),
        out_shape=jax.ShapeDtypeStruct((T, D), F32),
        scratch_shapes=[pltpu.VMEM((te, tt), F32), pltpu.VMEM((te, tt), BF16), pltpu.VMEM((D, tt), F32)],
        compiler_params=_cparams(("arbitrary", "arbitrary")),
        name="peer_dense",
    )(x2, h, p1, p2, th, u, vt)


def _block_ones(width):
    idx = np.arange(GROUP) // width
    return jnp.asarray(idx[:, None] == idx[None, :], BF16)


def _pad_head_rows(w):
    w = w.reshape(N_HEADS, HEAD_DIM, -1)
    return jnp.concatenate([w, jnp.zeros_like(w)], axis=1).astype(BF16)


def _dil_bias(rel_bias, window, dil):
    ws = window // dil
    j = jnp.arange(DIL_BLOCK)[:, None]
    m = jnp.arange(2 * DIL_BLOCK)[None, :]
    sub = DIL_BLOCK + j - m
    band = (sub >= 0) & (sub <= ws)
    bias = rel_bias[_t5_bucket(dil * jnp.clip(sub, 0, ws))].transpose(2, 0, 1).astype(F32)
    return jnp.where(band[None], bias, NEG)


def _diff_bias(rel_bias, t, n_near):
    d = jnp.arange(n_near)[:, None, None] * t + jnp.arange(t)[None, :, None] - jnp.arange(t)[None, None, :]
    bias = rel_bias[_t5_bucket(d)].astype(F32) - rel_bias[REL_BUCKETS - 1].astype(F32)
    bias = jnp.where((d >= 0)[..., None], bias, NEG)
    return bias.transpose(3, 0, 1, 2)


def _s5_params(a_re, a_im, log_dt, b_re, b_im, c_re, c_im, d_skip):
    G, N, C = S5_GROUPS, S5_STATE, S5_CH
    dt = jnp.exp(log_dt.astype(F32))[:, None]
    ar, ai = a_re.astype(F32), a_im.astype(F32)
    mag = jnp.exp(dt * ar)
    abar_re, abar_im = mag * jnp.cos(dt * ai), mag * jnp.sin(dt * ai)
    den = ar * ar + ai * ai
    nr, ni = abar_re - 1.0, abar_im
    coef_re = (nr * ar + ni * ai) / den
    coef_im = (ni * ar - nr * ai) / den
    br, bi = b_re.astype(F32), b_im.astype(F32)
    bbar_re = coef_re[..., None] * br - coef_im[..., None] * bi
    bbar_im = coef_re[..., None] * bi + coef_im[..., None] * br
    eye = jnp.eye(G, dtype=F32)

    def b_blockdiag(bb):
        return jnp.einsum('gnc,gk->gckn', bb, eye).reshape(G * C, G * N).astype(BF16)

    def c_blockdiag(cc):
        return jnp.einsum('gcn,gk->gnkc', cc.astype(F32), eye).reshape(G * N, G * C).astype(BF16)

    return (abar_re.reshape(1, G * N), abar_im.reshape(1, G * N), b_blockdiag(bbar_re), b_blockdiag(bbar_im),
            c_blockdiag(c_re), c_blockdiag(c_im), d_skip.astype(F32).reshape(1, G * C))


def _pick(n, pref):
    for c in pref:
        if n % c == 0:
            return c
    raise ValueError("unsupported size %d" % n)


def kernel(x, w_in, b_forget, w_out, norm_mix, norm_ffn, fox_q_norm, fox_k_norm, dil_q_norm, dil_k_norm, diff_q_norm, diff_k_norm, diff_lambda, diff_out_norm, rel_bias, s5_a_re, s5_a_im, s5_log_dt, s5_b_re, s5_b_im, s5_c_re, s5_c_im, s5_d, s5_glu_w, s5_glu_b, peer_wq, peer_keys, peer_u, peer_v):
    B, S, D = x.shape
    depth = w_in.shape[0]
    T = B * S
    assert S % (DIL_BLOCK * DIL_PAIRS[-1][1]) == 0 and D % LANES == 0
    tm = _pick(S, (512, 256, 128))
    t_diff = _pick(S, (512, 256))
    tq_fox = _pick(S, (1024, 512, 256))
    tkd_fox = min(tq_fox, 512)
    tc = _pick(S, (128, 64))
    tt_sel = _pick(T, (512, 256))
    tt_peer = _pick(T, (512, 256))
    te = 1024

    n_sat = _bucket_saturation()
    n_near = min(-(-(n_sat + t_diff - 1) // t_diff), S // t_diff)
    rel_bias = rel_bias.astype(F32)
    dil_bias = [_dil_bias(rel_bias[:, :N_HEADS], w, r) for w, r in DIL_PAIRS]
    diff_bias = _diff_bias(rel_bias[:, N_HEADS:], t_diff, n_near) * LOG2E
    far = rel_bias[REL_BUCKETS - 1, N_HEADS:] * LOG2E
    far_hi = far.astype(BF16).astype(F32)
    lane = jnp.arange(LANES)
    aug = jnp.zeros((2 + N_HEADS, LANES), F32)
    aug = aug.at[0].set(jnp.where(lane == HEAD_DIM, 1.0, 0.0))
    aug = aug.at[1].set(jnp.where((lane == HEAD_DIM) | (lane == HEAD_DIM + 1), 1.0, 0.0))
    aug = aug.at[2:].set(jnp.where(lane[None] == HEAD_DIM, far_hi[:, None],
                                   jnp.where(lane[None] == HEAD_DIM + 1, (far - far_hi)[:, None], 0.0)))
    ones64, ones32 = _block_ones(HEAD_DIM), _block_ones(DIFF_HALF)
    tri = jnp.asarray(np.tril(np.ones((tm, tm), np.float32)), BF16)

    sizes = [GROUP, GROUP, GROUP, N_HEADS] + [GROUP] * 7
    starts = np.cumsum([0] + sizes)
    order = [0, 1, 2, 4, 5, 6, 7, 8, 9, 10]

    x = x.astype(F32)
    for layer in range(depth):
        wl = w_in[layer]
        w_packed = jnp.concatenate(
            [wl[:, starts[k]:starts[k + 1]] for k in order]
            + [wl[:, starts[3]:starts[4]], jnp.zeros((D, LANES - N_HEADS), wl.dtype)], axis=1).astype(BF16)
        bf = jnp.zeros((1, LANES), F32).at[0, :N_HEADS].set(b_forget[layer].astype(F32))
        gains = jnp.stack([
            jnp.tile(fox_q_norm[layer].astype(F32), N_HEADS) * (HEAD_DIM ** -0.5 * LOG2E),
            jnp.tile(fox_k_norm[layer].astype(F32), N_HEADS),
            jnp.tile(dil_q_norm[layer].astype(F32), N_HEADS) * HEAD_DIM ** -0.5,
            jnp.tile(dil_k_norm[layer].astype(F32), N_HEADS),
            jnp.tile(diff_q_norm[layer].astype(F32), 2 * N_HEADS) * (DIFF_HALF ** -0.5 * LOG2E),
            jnp.tile(diff_k_norm[layer].astype(F32), 2 * N_HEADS),
            jnp.zeros((GROUP,), F32), jnp.zeros((GROUP,), F32)])
        (fq, fk, fv, cq1, cq2, ck, cv, dq, dk, dv, su) = _inproj(
            x, norm_mix[layer].astype(F32)[None], w_packed, bf, gains, ones64, ones32, tri, aug, tm)

        fox_o = _fox_attention(fq, fk, fv, tq_fox, tq_fox, tkd_fox)

        lam_init = 0.8 - 0.6 * math.exp(-0.3 * layer)
        lam_p = diff_lambda[layer].astype(F32)
        lam = jnp.exp(jnp.sum(lam_p[0] * lam_p[1])) - jnp.exp(jnp.sum(lam_p[2] * lam_p[3])) + lam_init
        out_gain = jnp.zeros((1, LANES), F32).at[0, :HEAD_DIM].set(diff_out_norm[layer].astype(F32))
        diff_o = _diff_attention(jnp.stack([lam, 1.0 - lam_init]).astype(F32), cq1, cq2, ck, cv, diff_bias,
                                 out_gain, t_diff)

        branches = [_dil_branch(dq, dk, dv, dil_bias[n], r) for n, (_, r) in enumerate(DIL_PAIRS)]
        dil_o = _dil_combine([b[0] for b in branches], [b[1] for b in branches], tm)

        s5p = _s5_params(s5_a_re[layer], s5_a_im[layer], s5_log_dt[layer], s5_b_re[layer], s5_b_im[layer],
                         s5_c_re[layer], s5_c_im[layer], s5_d[layer])
        s5_o = _s5(su.reshape(S * B, GROUP), *s5p, s5_glu_w[layer].astype(BF16),
                   s5_glu_b[layer].astype(F32)[None], B, tc).reshape(S, B * GROUP)

        wo = w_out[layer]
        x = _outproj(x, fox_o, dil_o, diff_o, s5_o,
                     _pad_head_rows(wo[0:GROUP]), wo[GROUP:2 * GROUP].astype(BF16),
                     _pad_head_rows(wo[2 * GROUP:3 * GROUP]), wo[3 * GROUP:].astype(BF16), tm)

        x2 = x.reshape(T, D)
        h, p1, p2, th = _peer_select(x2, norm_ffn[layer].astype(F32)[None], peer_wq[layer].astype(BF16),
                                     peer_keys[layer].astype(BF16), tt_sel)
        x = _peer_dense(x2, h, p1, p2, th, peer_u[layer].astype(BF16), peer_v[layer].astype(BF16).T,
                        tt_peer, te).reshape(B, S, D)
    return x
```

```python
import functools
import math

import numpy as np
import jax
import jax.numpy as jnp
from jax import lax
from jax.experimental import pallas as pl
from jax.experimental.pallas import tpu as pltpu

F32 = jnp.float32
BF16 = jnp.bfloat16

LANES = 128
GROUP = 256
HEAD_DIM = 64
N_HEADS = GROUP // HEAD_DIM
DIFF_HALF = HEAD_DIM // 2
S5_CH = 16
S5_GROUPS = GROUP // S5_CH
S5_STATE = 64
S5_WIDE = S5_GROUPS * S5_STATE
DIL_BLOCK = 128
DIL_PAIRS = ((128, 1), (512, 4), (2048, 16))
REL_BUCKETS = 32
REL_MAX_EXACT = REL_BUCKETS // 2
REL_MAX_DIST = 2048
N_KEYS = 128
PEER_HEADS = 8
PEER_TOPK = 16
RMS_EPS = 1e-6
NEG = -1e30
LOG2E = math.log2(math.e)
GELU_A = -2.0 * math.sqrt(2.0 / math.pi) * LOG2E
GELU_B = GELU_A * 0.044715
PEER_SUB = 256
VMEM_LIMIT = 56 * 1024 * 1024

IN_COLS = 10 * GROUP + LANES
FORGET_COL = 10 * GROUP


def _cparams(sem):
    return pltpu.CompilerParams(dimension_semantics=sem, vmem_limit_bytes=VMEM_LIMIT)


def _dot(a, b):
    return jnp.dot(a, b, preferred_element_type=F32)


def _dot_nt(a, b):
    return lax.dot_general(a, b, (((1,), (1,)), ((), ())), preferred_element_type=F32)


def _split3(x):
    hi = x.astype(BF16)
    r = x - hi.astype(F32)
    mid = r.astype(BF16)
    lo = (r - mid.astype(F32)).astype(BF16)
    return hi, mid, lo


def _gelu(x):
    return 0.5 * x * (1.0 + jnp.tanh(math.sqrt(2.0 / math.pi) * (x + 0.044715 * (x * x * x))))


def _t5_bucket(dist):
    n = jnp.maximum(dist, 0)
    nf = jnp.maximum(n, REL_MAX_EXACT).astype(F32)
    large = REL_MAX_EXACT + (jnp.log(nf / REL_MAX_EXACT) / math.log(REL_MAX_DIST / REL_MAX_EXACT)
                             * (REL_BUCKETS - REL_MAX_EXACT)).astype(jnp.int32)
    large = jnp.minimum(large, REL_BUCKETS - 1)
    return jnp.where(n < REL_MAX_EXACT, n, large)


def _bucket_saturation():
    n = REL_MAX_EXACT * (REL_MAX_DIST / REL_MAX_EXACT) ** ((REL_BUCKETS - 1 - REL_MAX_EXACT)
                                                            / (REL_BUCKETS - REL_MAX_EXACT))
    return int(math.ceil(n)) + 2


def _inproj_kernel(x_ref, g_ref, w_ref, bf_ref, gains_ref, ones64_ref, ones32_ref, tri_ref, aug_ref,
                   fq_ref, fk_ref, fv_ref, cq1_ref, cq2_ref, ck_ref, cv_ref,
                   dq_ref, dk_ref, dv_ref, su_ref, carry_ref):
    @pl.when(pl.program_id(1) == 0)
    def _():
        carry_ref[...] = jnp.zeros_like(carry_ref)

    x = x_ref[0]
    tm = x.shape[0]
    h = x * lax.rsqrt(jnp.mean(x * x, axis=-1, keepdims=True) + RMS_EPS) * g_ref[...]
    z = _dot(h.astype(BF16), w_ref[...])

    def group(idx):
        return z[:, idx * GROUP:(idx + 1) * GROUP]

    def head_norm(zg, ones_ref, width, gain_row):
        ss = _dot((zg * zg).astype(BF16), ones_ref[...])
        return zg * lax.rsqrt(ss * (1.0 / width) + RMS_EPS) * gains_ref[gain_row:gain_row + 1, :]

    lane = lax.broadcasted_iota(jnp.int32, (tm, LANES), 1)

    def head_chunk(zg, hh):
        c = zg[:, (hh // 2) * LANES:(hh // 2 + 1) * LANES]
        if hh % 2:
            c = pltpu.roll(c, HEAD_DIM, axis=1)
        return c

    f = z[:, FORGET_COL:FORGET_COL + LANES] + bf_ref[...]
    ls = jnp.minimum(f, 0.0) - jnp.log1p(jnp.exp(-jnp.abs(f)))
    p0, p1, p2 = _split3(ls)
    tri = tri_ref[...]
    cum = (_dot(tri, p0) + _dot(tri, p1)) + _dot(tri, p2) + carry_ref[...]
    carry_ref[...] = cum[tm - 1:tm, :]

    fq = head_norm(group(0), ones64_ref, HEAD_DIM, 0)
    fk = head_norm(group(1), ones64_ref, HEAD_DIM, 1)
    fv = group(2)
    dq = head_norm(group(3), ones64_ref, HEAD_DIM, 2)
    dk = head_norm(group(4), ones64_ref, HEAD_DIM, 3)
    cq = head_norm(group(6), ones32_ref, DIFF_HALF, 4)
    ck = head_norm(group(7), ones32_ref, DIFF_HALF, 5)
    cv = group(8)

    dq_ref[0] = dq.astype(BF16)
    dk_ref[0] = dk.astype(BF16)
    dv_ref[0] = group(5).astype(BF16)
    su_ref[...] = group(9)

    one = jnp.ones((tm, LANES), F32)
    zero = jnp.zeros((tm, LANES), F32)
    for hh in range(N_HEADS):
        c = jnp.broadcast_to(cum[:, hh:hh + 1], (tm, LANES)) * LOG2E
        c0, c1, c2 = _split3(c)
        c0, c1, c2 = c0.astype(F32), c1.astype(F32), c2.astype(F32)
        qa = jnp.where(lane == 64, c0, jnp.where(lane == 65, c1, jnp.where(lane == 66, c2,
             jnp.where(lane < 70, one, zero))))
        ka = jnp.where(lane == 67, -c0, jnp.where(lane == 68, -c1, jnp.where(lane == 69, -c2,
             jnp.where(lane < 70, one, zero))))
        fq_ref[0, hh] = jnp.where(lane < HEAD_DIM, head_chunk(fq, hh), qa).astype(BF16)
        fk_ref[0, hh] = jnp.where(lane < HEAD_DIM, head_chunk(fk, hh), ka).astype(BF16)
        vaug = aug_ref[0:1, :]
        fv_ref[0, hh] = jnp.where(lane < HEAD_DIM, head_chunk(fv, hh), vaug).astype(BF16)
        qaug = aug_ref[1:2, :]
        cqc = head_chunk(cq, hh)
        cq1_ref[0, hh] = jnp.where(lane < DIFF_HALF, cqc, qaug).astype(BF16)
        cq2_ref[0, hh] = jnp.where((lane >= DIFF_HALF) & (lane < HEAD_DIM), cqc, qaug).astype(BF16)
        kaug = aug_ref[2 + hh:3 + hh, :]
        ck_ref[0, hh] = jnp.where(lane < HEAD_DIM, head_chunk(ck, hh), kaug).astype(BF16)
        cv_ref[0, hh] = jnp.where(lane < HEAD_DIM, head_chunk(cv, hh), vaug).astype(BF16)


def _inproj(x, g, w, bf, gains, ones64, ones32, tri, aug, tm):
    B, S, D = x.shape
    nt = S // tm
    hm = jax.ShapeDtypeStruct((B, N_HEADS, S, LANES), BF16)
    nat = jax.ShapeDtypeStruct((B, S, GROUP), BF16)
    hm_spec = pl.BlockSpec((1, N_HEADS, tm, LANES), lambda b, i: (b, 0, i, 0))
    nat_spec = pl.BlockSpec((1, tm, GROUP), lambda b, i: (b, i, 0))

    def full(a):
        return pl.BlockSpec(a.shape, lambda b, i: (0,) * a.ndim)

    return pl.pallas_call(
        _inproj_kernel,
        grid=(B, nt),
        in_specs=[pl.BlockSpec((1, tm, D), lambda b, i: (b, i, 0)),
                  full(g), full(w), full(bf), full(gains), full(ones64), full(ones32), full(tri), full(aug)],
        out_specs=[hm_spec] * 7 + [nat_spec] * 3 + [pl.BlockSpec((tm, GROUP), lambda b, i: (i, b))],
        out_shape=[hm] * 7 + [nat] * 3 + [jax.ShapeDtypeStruct((S, B * GROUP), F32)],
        scratch_shapes=[pltpu.VMEM((1, LANES), F32)],
        compiler_params=_cparams(("arbitrary", "arbitrary")),
        name="inproj",
    )(x, g, w, bf, gains, ones64, ones32, tri, aug)


def _flash_step(q, k, v, m, acc, bias=None):
    s = _dot_nt(q, k)
    if bias is not None:
        s = s + bias
    m_new = jnp.maximum(m, jnp.max(s, axis=1, keepdims=True))
    alpha = jnp.exp2(m - m_new)
    p = jnp.exp2(s - m_new)
    acc = alpha * acc + _dot(p.astype(BF16), v)
    return m_new, acc


def _fox_kernel(q_ref, k_ref, v_ref, o_ref, *, tq, tk, tkd):
    i = pl.program_id(2)
    q = q_ref[0, 0]

    def kv(off, size):
        return k_ref[0, 0, pl.ds(off, size), :], v_ref[0, 0, pl.ds(off, size), :]

    def body(j, carry):
        m, acc = carry
        k, v = kv(pl.multiple_of(j * tk, tk), tk)
        return _flash_step(q, k, v, m, acc)

    m0 = jnp.full((tq, 1), NEG, F32)
    acc0 = jnp.zeros((tq, LANES), F32)
    m, acc = lax.fori_loop(0, i * (tq // tk), body, (m0, acc0))
    for d in range(tq // tkd):
        rows = tq - d * tkd
        k, v = kv(pl.multiple_of(i * tq + d * tkd, tkd), tkd)
        row = lax.broadcasted_iota(jnp.int32, (rows, tkd), 0)
        col = lax.broadcasted_iota(jnp.int32, (rows, tkd), 1)
        mask = jnp.where(col <= row, 0.0, NEG)
        m_d, acc_d = _flash_step(q[d * tkd:], k, v, m[d * tkd:], acc[d * tkd:], mask)
        if d:
            m = jnp.concatenate([m[:d * tkd], m_d], axis=0)
            acc = jnp.concatenate([acc[:d * tkd], acc_d], axis=0)
        else:
            m, acc = m_d, acc_d
    den = jnp.broadcast_to(acc[:, HEAD_DIM:HEAD_DIM + 1], (tq, LANES))
    o_ref[0, 0] = (acc / den).astype(BF16)


def _fox_attention(q, k, v, tq, tk, tkd):
    B, H, S, _ = q.shape
    return pl.pallas_call(
        functools.partial(_fox_kernel, tq=tq, tk=tk, tkd=tkd),
        grid=(B, H, S // tq),
        in_specs=[pl.BlockSpec((1, 1, tq, LANES), lambda b, h, i: (b, h, i, 0)),
                  pl.BlockSpec((1, 1, S, LANES), lambda b, h, i: (b, h, 0, 0)),
                  pl.BlockSpec((1, 1, S, LANES), lambda b, h, i: (b, h, 0, 0))],
        out_specs=pl.BlockSpec((1, 1, tq, LANES), lambda b, h, i: (b, h, i, 0)),
        out_shape=jax.ShapeDtypeStruct((B, H, S, LANES), BF16),
        compiler_params=_cparams(("arbitrary", "arbitrary", "arbitrary")),
        name="fox_attention",
    )(q, k, v)


def _diff_kernel(lam_ref, q1_ref, q2_ref, k_ref, v_ref, bias_ref, gain_ref, o_ref, *, t, n_near):
    i = pl.program_id(2)
    q = jnp.concatenate([q1_ref[0, 0], q2_ref[0, 0]], axis=0)
    n_far = jnp.maximum(i - (n_near - 1), 0)

    def kv(j):
        off = pl.multiple_of(j * t, t)
        return k_ref[0, 0, pl.ds(off, t), :], v_ref[0, 0, pl.ds(off, t), :]

    def far(j, carry):
        m, acc = carry
        k, v = kv(j)
        return _flash_step(q, k, v, m, acc)

    def near(j, carry):
        m, acc = carry
        k, v = kv(j)
        b = bias_ref[0, i - j]
        return _flash_step(q, k, v, m, acc, jnp.concatenate([b, b], axis=0))

    m0 = jnp.full((2 * t, 1), NEG, F32)
    a0 = jnp.zeros((2 * t, LANES), F32)
    carry = lax.fori_loop(0, n_far, far, (m0, a0))
    _, acc = lax.fori_loop(n_far, i + 1, near, carry)
    a1, a2 = acc[:t], acc[t:]
    d1 = jnp.broadcast_to(a1[:, HEAD_DIM:HEAD_DIM + 1], (t, LANES))
    d2 = jnp.broadcast_to(a2[:, HEAD_DIM:HEAD_DIM + 1], (t, LANES))
    lane = lax.broadcasted_iota(jnp.int32, (t, LANES), 1)
    o = jnp.where(lane < HEAD_DIM, a1 / d1 - lam_ref[0] * (a2 / d2), 0.0)
    ms = jnp.sum(o * o, axis=1, keepdims=True) * (1.0 / HEAD_DIM)
    o_ref[0, 0] = (o * lax.rsqrt(ms + RMS_EPS) * gain_ref[...] * lam_ref[1]).astype(BF16)


def _diff_attention(lam, q1, q2, k, v, bias, gain, t):
    B, H, S, _ = q1.shape
    n_near = bias.shape[1]
    qspec = pl.BlockSpec((1, 1, t, LANES), lambda b, h, i: (b, h, i, 0))
    kspec = pl.BlockSpec((1, 1, S, LANES), lambda b, h, i: (b, h, 0, 0))
    return pl.pallas_call(
        functools.partial(_diff_kernel, t=t, n_near=n_near),
        grid=(B, H, S // t),
        in_specs=[pl.BlockSpec(memory_space=pltpu.SMEM), qspec, qspec, kspec, kspec,
                  pl.BlockSpec((1, n_near, t, t), lambda b, h, i: (h, 0, 0, 0)),
                  pl.BlockSpec((1, LANES), lambda b, h, i: (0, 0))],
        out_specs=qspec,
        out_shape=jax.ShapeDtypeStruct((B, H, S, LANES), BF16),
        compiler_params=_cparams(("arbitrary", "arbitrary", "arbitrary")),
        name="diff_attention",
    )(lam, q1, q2, k, v, bias, gain)


def _dil_kernel(q_ref, kp_ref, kc_ref, vp_ref, vc_ref, bias_ref, num_ref, st_ref):
    blk = pl.program_id(1)
    q = q_ref[0]
    kp, kc, vp, vc = kp_ref[0], kc_ref[0], vp_ref[0], vc_ref[0]
    w = q.shape[0]
    lane = lax.broadcasted_iota(jnp.int32, (w, GROUP), 1)
    lane_s = lax.broadcasted_iota(jnp.int32, (w, LANES), 1)
    prev_ok = jnp.where(blk > 0, 0.0, NEG)
    num = jnp.zeros((w, GROUP), F32)
    st = jnp.zeros((w, LANES), F32)
    for hh in range(N_HEADS):
        in_head = (lane >= hh * HEAD_DIM) & (lane < (hh + 1) * HEAD_DIM)
        qh = jnp.where(in_head, q, jnp.zeros_like(q))
        sp = _dot_nt(qh, kp) + bias_ref[hh, :, 0:DIL_BLOCK] + prev_ok
        sc = _dot_nt(qh, kc) + bias_ref[hh, :, DIL_BLOCK:2 * DIL_BLOCK]
        m = jnp.maximum(jnp.max(sp, axis=1, keepdims=True), jnp.max(sc, axis=1, keepdims=True))
        pp = jnp.exp(sp - m)
        pc = jnp.exp(sc - m)
        den = jnp.sum(pp, axis=1, keepdims=True) + jnp.sum(pc, axis=1, keepdims=True)
        o = _dot(pp.astype(BF16), vp) + _dot(pc.astype(BF16), vc)
        num = jnp.where(in_head, o, num)
        st = jnp.where(lane_s == hh, den, jnp.where(lane_s == N_HEADS + hh, m, st))
    num_ref[0] = num
    st_ref[0] = st


def _dil_branch(q, k, v, bias, dil):
    B, S, _ = q.shape
    L = S // dil
    nb = L // DIL_BLOCK
    qv, kv_, vv = (a.reshape(B, L, dil * GROUP) for a in (q, k, v))
    cur = pl.BlockSpec((1, DIL_BLOCK, GROUP), lambda b, n, r: (b, n, r))
    prev = pl.BlockSpec((1, DIL_BLOCK, GROUP), lambda b, n, r: (b, jnp.maximum(n - 1, 0), r))
    num, st = pl.pallas_call(
        _dil_kernel,
        grid=(B, nb, dil),
        in_specs=[cur, prev, cur, prev, cur,
                  pl.BlockSpec(bias.shape, lambda b, n, r: (0, 0, 0))],
        out_specs=[cur, pl.BlockSpec((1, DIL_BLOCK, LANES), lambda b, n, r: (b, n, r))],
        out_shape=[jax.ShapeDtypeStruct((B, L, dil * GROUP), F32),
                   jax.ShapeDtypeStruct((B, L, dil * LANES), F32)],
        compiler_params=_cparams(("arbitrary", "arbitrary", "arbitrary")),
        name="dilated_branch_%d" % dil,
    )(qv, kv_, kv_, vv, vv, bias)
    return num.reshape(B, S, GROUP), st.reshape(B, S, LANES)


def _dil_combine_kernel(n0_ref, n1_ref, n2_ref, s0_ref, s1_ref, s2_ref, o_ref):
    nums = [r[0] for r in (n0_ref, n1_ref, n2_ref)]
    sts = [r[0] for r in (s0_ref, s1_ref, s2_ref)]
    tm = nums[0].shape[0]
    lane = lax.broadcasted_iota(jnp.int32, (tm, GROUP), 1)
    out = jnp.zeros((tm, GROUP), F32)
    for hh in range(N_HEADS):
        ms = [s[:, N_HEADS + hh:N_HEADS + hh + 1] for s in sts]
        ds = [s[:, hh:hh + 1] for s in sts]
        mx = jnp.maximum(jnp.maximum(ms[0], ms[1]), ms[2])
        ws = [jnp.exp(m - mx) for m in ms]
        den = ws[0] * ds[0] + ws[1] * ds[1] + ws[2] * ds[2]
        o = (ws[0] * nums[0] + ws[1] * nums[1] + ws[2] * nums[2]) / den
        out = jnp.where((lane >= hh * HEAD_DIM) & (lane < (hh + 1) * HEAD_DIM), o, out)
    o_ref[0] = out.astype(BF16)


def _dil_combine(nums, sts, tm):
    B, S, _ = nums[0].shape
    nspec = pl.BlockSpec((1, tm, GROUP), lambda b, i: (b, i, 0))
    sspec = pl.BlockSpec((1, tm, LANES), lambda b, i: (b, i, 0))
    return pl.pallas_call(
        _dil_combine_kernel,
        grid=(B, S // tm),
        in_specs=[nspec] * 3 + [sspec] * 3,
        out_specs=nspec,
        out_shape=jax.ShapeDtypeStruct((B, S, GROUP), BF16),
        compiler_params=_cparams(("arbitrary", "arbitrary")),
        name="dilated_combine",
    )(*nums, *sts)


def _s5_kernel(u_ref, are_ref, aim_ref, bre_ref, bim_ref, cre_ref, cim_ref, d_ref, gw_ref, gb_ref,
               o_ref, xr_ref, xi_ref, bur_ref, bui_ref, sr_ref, si_ref, *, nb, tc):
    @pl.when(pl.program_id(0) == 0)
    def _():
        sr_ref[...] = jnp.zeros_like(sr_ref)
        si_ref[...] = jnp.zeros_like(si_ref)

    u = u_ref[...]
    ub = u.astype(BF16)
    bur_ref[...] = _dot(ub, bre_ref[...])
    bui_ref[...] = _dot(ub, bim_ref[...])
    ar = jnp.broadcast_to(are_ref[...], (nb, S5_WIDE))
    ai = jnp.broadcast_to(aim_ref[...], (nb, S5_WIDE))

    def step(t, carry):
        xr, xi = carry
        rows = pl.ds(pl.multiple_of(t * nb, nb), nb)
        nr = ar * xr - ai * xi + bur_ref[rows, :]
        ni = ar * xi + ai * xr + bui_ref[rows, :]
        xr_ref[rows, :] = nr
        xi_ref[rows, :] = ni
        return nr, ni

    xr, xi = lax.fori_loop(0, tc, step, (sr_ref[...], si_ref[...]))
    sr_ref[...] = xr
    si_ref[...] = xi
    y = (_dot(xr_ref[...].astype(BF16), cre_ref[...]) - _dot(xi_ref[...].astype(BF16), cim_ref[...])
         + d_ref[...] * u)
    g = _gelu(y)
    gate = _dot(g.astype(BF16), gw_ref[...]) + gb_ref[...]
    o_ref[...] = (g * (1.0 / (1.0 + jnp.exp(-gate)))).astype(BF16)


def _s5(u2, are, aim, bre, bim, cre, cim, d, gw, gb, nb, tc):
    rows = u2.shape[0]
    tr = tc * nb

    def full(a):
        return pl.BlockSpec(a.shape, lambda i: (0,) * a.ndim)

    consts = (are, aim, bre, bim, cre, cim, d, gw, gb)
    return pl.pallas_call(
        functools.partial(_s5_kernel, nb=nb, tc=tc),
        grid=(rows // tr,),
        in_specs=[pl.BlockSpec((tr, GROUP), lambda i: (i, 0))] + [full(a) for a in consts],
        out_specs=pl.BlockSpec((tr, GROUP), lambda i: (i, 0)),
        out_shape=jax.ShapeDtypeStruct((rows, GROUP), BF16),
        scratch_shapes=[pltpu.VMEM((tr, S5_WIDE), F32)] * 4 + [pltpu.VMEM((nb, S5_WIDE), F32)] * 2,
        compiler_params=_cparams(("arbitrary",)),
        name="s5_mixer",
    )(u2, *consts)


def _outproj_kernel(x_ref, fox_ref, dil_ref, dif_ref, s5_ref, wf_ref, wd_ref, wc_ref, ws_ref, o_ref):
    acc = x_ref[0] + _dot(dil_ref[0], wd_ref[...]) + _dot(s5_ref[...], ws_ref[...])
    for hh in range(N_HEADS):
        acc = acc + _dot(fox_ref[0, hh], wf_ref[hh]) + _dot(dif_ref[0, hh], wc_ref[hh])
    o_ref[0] = acc


def _outproj(x, fox, dil, dif, s5, wf, wd, wc, ws, tm):
    B, S, D = x.shape
    xspec = pl.BlockSpec((1, tm, D), lambda b, i: (b, i, 0))
    hspec = pl.BlockSpec((1, N_HEADS, tm, LANES), lambda b, i: (b, 0, i, 0))

    def full(a):
        return pl.BlockSpec(a.shape, lambda b, i: (0,) * a.ndim)

    return pl.pallas_call(
        _outproj_kernel,
        grid=(B, S // tm),
        in_specs=[xspec, hspec, pl.BlockSpec((1, tm, GROUP), lambda b, i: (b, i, 0)), hspec,
                  pl.BlockSpec((tm, GROUP), lambda b, i: (i, b)),
                  full(wf), full(wd), full(wc), full(ws)],
        out_specs=xspec,
        out_shape=jax.ShapeDtypeStruct((B, S, D), F32),
        compiler_params=_cparams(("arbitrary", "arbitrary")),
        name="outproj",
    )(x, fox, dil, dif, s5, wf, wd, wc, ws)


def _staircase(n):
    return [(a, b) for a in range(n) for b in range(n) if (a + 1) * (b + 1) <= n]


def _peer_select_kernel(x_ref, g_ref, wq_ref, keys_ref, h_ref, p1_ref, n1_ref, p2_ref, r2_ref):
    x = x_ref[...]
    tt = x.shape[0]
    h = (x * lax.rsqrt(jnp.mean(x * x, axis=-1, keepdims=True) + RMS_EPS) * g_ref[...]).astype(BF16)
    h_ref[...] = h
    q = _dot(h, wq_ref[...]).astype(BF16)
    n_top = PEER_TOPK + 1
    pairs = _staircase(n_top)
    for hh in range(PEER_HEADS):
        tops = []
        scores = []
        rank = jnp.full((N_KEYS, tt), float(N_KEYS), F32)
        for half in range(2):
            c = (hh * 2 + half) * N_KEYS
            s = _dot_nt(keys_ref[hh, half], q[:, c:c + N_KEYS])
            scores.append(s)
            vals = []
            for kk in range(n_top):
                mx = jnp.max(s, axis=0, keepdims=True)
                vals.append(mx)
                hit = s == mx
                if half == 1:
                    rank = jnp.where(hit, float(kk), rank)
                s = jnp.where(hit, NEG, s)
            tops.append(vals)
        cand = jnp.concatenate([tops[0][a] + tops[1][b] for a, b in pairs]
                               + [jnp.full((8 - len(pairs) % 8, tt), NEG, F32)], axis=0)
        best = []
        for _ in range(n_top):
            mx = jnp.max(cand, axis=0, keepdims=True)
            best.append(mx)
            cand = jnp.where(cand == mx, NEG, cand)
        z = jnp.zeros((1, tt), F32)
        for kk in range(PEER_TOPK):
            z = z + jnp.exp(best[kk] - best[0])
        cut = 0.5 * (best[PEER_TOPK - 1] + best[PEER_TOPK])
        n1 = jnp.zeros((N_KEYS, tt), F32)
        for b in range(PEER_TOPK):
            n1 = n1 + jnp.where(scores[0] >= cut - tops[1][b], 1.0, 0.0)
        p1_ref[hh] = jnp.exp(scores[0] - tops[0][0]) * (1.0 / z)
        n1_ref[hh] = n1
        p2_ref[hh] = pltpu.bitcast(jnp.exp(scores[1] - tops[1][0]).astype(BF16), jnp.uint32)
        r2_ref[hh] = pltpu.bitcast(rank.astype(BF16), jnp.uint32)


def _peer_select(x2, g, wq, keys, tt):
    T, D = x2.shape
    rows = pl.BlockSpec((PEER_HEADS, N_KEYS, tt), lambda i: (0, 0, i))
    words = pl.BlockSpec((PEER_HEADS, N_KEYS // 2, tt), lambda i: (0, 0, i))
    return pl.pallas_call(
        _peer_select_kernel,
        grid=(T // tt,),
        in_specs=[pl.BlockSpec((tt, D), lambda i: (i, 0)),
                  pl.BlockSpec(g.shape, lambda i: (0, 0)),
                  pl.BlockSpec(wq.shape, lambda i: (0, 0)),
                  pl.BlockSpec(keys.shape, lambda i: (0, 0, 0, 0))],
        out_specs=[pl.BlockSpec((tt, D), lambda i: (i, 0)), rows, rows, words, words],
        out_shape=[jax.ShapeDtypeStruct((T, D), BF16),
                   jax.ShapeDtypeStruct((PEER_HEADS, N_KEYS, T), F32),
                   jax.ShapeDtypeStruct((PEER_HEADS, N_KEYS, T), F32),
                   jax.ShapeDtypeStruct((PEER_HEADS, N_KEYS // 2, T), jnp.uint32),
                   jax.ShapeDtypeStruct((PEER_HEADS, N_KEYS // 2, T), jnp.uint32)],
        compiler_params=_cparams(("arbitrary",)),
        name="peer_select",
    )(x2, g, wq, keys)


def _gelu_gate(x, g):
    e = jnp.exp2(x * (GELU_A + GELU_B * (x * x)))
    return x * g / (1.0 + e)


def _peer_dense_kernel(x_ref, h_ref, p1_ref, n1_ref, p2_ref, r2_ref, u_ref, vt_ref, o_ref,
                       a_ref, w_ref, acc_ref, *, te, tt, n_e):
    s = pl.program_id(0)

    @pl.when(s == 0)
    def _():
        a_ref[...] = jnp.zeros_like(a_ref)
        w_ref[...] = jnp.zeros_like(w_ref)
        acc_ref[...] = jnp.zeros_like(acc_ref)

    slot_a = s % 2
    slot_b = 1 - slot_a
    j_b = jnp.maximum(s - 1, 0) % n_e
    j_c = jnp.maximum(s - 2, 0) % n_e

    h = h_ref[...]
    for q in range(te // PEER_SUB):
        a_ref[slot_a, q * PEER_SUB:(q + 1) * PEER_SUB, :] = _dot_nt(u_ref[q * PEER_SUB:(q + 1) * PEER_SUB, :], h)

    acc_ref[...] = jnp.where(j_c == 0, 0.0, acc_ref[...]) + _dot(vt_ref[...], w_ref[slot_a])

    rows_per_step = te // N_KEYS
    rows = pl.ds(pl.multiple_of(j_b * rows_per_step, rows_per_step), rows_per_step)
    for r in range(rows_per_step):
        for c in range(tt // LANES):
            cols = slice(c * LANES, (c + 1) * LANES)
            g = jnp.zeros((N_KEYS, LANES), BF16)
            for hh in range(PEER_HEADS):
                gate = jnp.broadcast_to(p1_ref[hh, rows, cols][r:r + 1], (N_KEYS, LANES)).astype(BF16)
                count = jnp.broadcast_to(n1_ref[hh, rows, cols][r:r + 1], (N_KEYS, LANES)).astype(BF16)
                r2 = pltpu.bitcast(r2_ref[hh, :, cols], BF16)
                p2 = pltpu.bitcast(p2_ref[hh, :, cols], BF16)
                sel = jnp.where(r2 < count, p2, jnp.zeros((), BF16))
                g = g + sel * gate
            a = a_ref[slot_b, r * N_KEYS:(r + 1) * N_KEYS, cols]
            w_ref[slot_b, r * N_KEYS:(r + 1) * N_KEYS, cols] = _gelu_gate(a, g.astype(F32)).astype(BF16)

    @pl.when((j_c == n_e - 1) & (s >= 2))
    def _():
        o_ref[...] = x_ref[...] + acc_ref[...].T


def _peer_dense(x2, h, p1, n1, p2, r2, u, vt, tt, te):
    T, D = x2.shape
    n_e = u.shape[0] // te
    last = (T // tt) * n_e - 1

    def tok(lag):
        return lambda s: jnp.clip(s - lag, 0, last) // n_e

    def tile(lag):
        return lambda s: jnp.clip(s - lag, 0, last) % n_e

    rows = pl.BlockSpec((PEER_HEADS, N_KEYS, tt), lambda s: (0, 0, tok(1)(s)))
    words = pl.BlockSpec((PEER_HEADS, N_KEYS // 2, tt), lambda s: (0, 0, tok(1)(s)))
    return pl.pallas_call(
        functools.partial(_peer_dense_kernel, te=te, tt=tt, n_e=n_e),
        grid=(last + 3,),
        in_specs=[pl.BlockSpec((tt, D), lambda s: (tok(2)(s), 0)),
                  pl.BlockSpec((tt, D), lambda s: (tok(0)(s), 0)),
                  rows, rows, words, words,
                  pl.BlockSpec((te, D), lambda s: (tile(0)(s), 0)),
                  pl.BlockSpec((D, te), lambda s: (0, tile(2)(s)))],
        out_specs=pl.BlockSpec((tt, D), lambda s: (tok(2)(s), 0)),
        out_shape=jax.ShapeDtypeStruct((T, D), F32),
        scratch_shapes=[pltpu.VMEM((2, te, tt), F32), pltpu.VMEM((2, te, tt), BF16), pltpu.VMEM((D, tt), F32)],
        compiler_params=_cparams(("arbitrary",)),
        name="peer_dense",
    )(x2, h, p1, n1, p2, r2, u, vt)


def _block_ones(width):
    idx = np.arange(GROUP) // width
    return jnp.asarray(idx[:, None] == idx[None, :], BF16)


def _pad_head_rows(w):
    w = w.reshape(N_HEADS, HEAD_DIM, -1)
    return jnp.concatenate([w, jnp.zeros_like(w)], axis=1).astype(BF16)


def _dil_bias(rel_bias, window, dil):
    ws = window // dil
    j = jnp.arange(DIL_BLOCK)[:, None]
    m = jnp.arange(2 * DIL_BLOCK)[None, :]
    sub = DIL_BLOCK + j - m
    band = (sub >= 0) & (sub <= ws)
    bias = rel_bias[_t5_bucket(dil * jnp.clip(sub, 0, ws))].transpose(2, 0, 1).astype(F32)
    return jnp.where(band[None], bias, NEG)


def _diff_bias(rel_bias, t, n_near):
    dist = jnp.arange(-(t - 1), n_near * t + 2)
    b = rel_bias[_t5_bucket(dist)].astype(F32) - rel_bias[REL_BUCKETS - 1].astype(F32)
    b = jnp.where((dist >= 0)[:, None], b, NEG).T
    rev = b[:, ::-1]
    n = b.shape[1]
    tiles = []
    for delta in range(n_near):
        start = n - 1 - (delta * t + 2 * t - 2)
        u = jnp.pad(rev[:, start:start + 2 * t - 1], ((0, 0), (0, 1)))
        flat = jnp.tile(u, (1, t))[:, t - 1:t - 1 + t * (2 * t - 1)]
        tiles.append(flat.reshape(-1, t, 2 * t - 1)[:, :, :t])
    return jnp.stack(tiles, axis=1)


def _s5_params(a_re, a_im, log_dt, b_re, b_im, c_re, c_im, d_skip):
    G, N, C = S5_GROUPS, S5_STATE, S5_CH
    dt = jnp.exp(log_dt.astype(F32))[:, None]
    ar, ai = a_re.astype(F32), a_im.astype(F32)
    mag = jnp.exp(dt * ar)
    abar_re, abar_im = mag * jnp.cos(dt * ai), mag * jnp.sin(dt * ai)
    den = ar * ar + ai * ai
    nr, ni = abar_re - 1.0, abar_im
    coef_re = (nr * ar + ni * ai) / den
    coef_im = (ni * ar - nr * ai) / den
    br, bi = b_re.astype(F32), b_im.astype(F32)
    bbar_re = coef_re[..., None] * br - coef_im[..., None] * bi
    bbar_im = coef_re[..., None] * bi + coef_im[..., None] * br
    eye = jnp.eye(G, dtype=F32)

    def b_blockdiag(bb):
        return jnp.einsum('gnc,gk->gckn', bb, eye).reshape(G * C, G * N).astype(BF16)

    def c_blockdiag(cc):
        return jnp.einsum('gcn,gk->gnkc', cc.astype(F32), eye).reshape(G * N, G * C).astype(BF16)

    return (abar_re.reshape(1, G * N), abar_im.reshape(1, G * N), b_blockdiag(bbar_re), b_blockdiag(bbar_im),
            c_blockdiag(c_re), c_blockdiag(c_im), d_skip.astype(F32).reshape(1, G * C))


def _pick(n, pref):
    for c in pref:
        if n % c == 0:
            return c
    raise ValueError("unsupported size %d" % n)


def kernel(x, w_in, b_forget, w_out, norm_mix, norm_ffn, fox_q_norm, fox_k_norm, dil_q_norm, dil_k_norm, diff_q_norm, diff_k_norm, diff_lambda, diff_out_norm, rel_bias, s5_a_re, s5_a_im, s5_log_dt, s5_b_re, s5_b_im, s5_c_re, s5_c_im, s5_d, s5_glu_w, s5_glu_b, peer_wq, peer_keys, peer_u, peer_v):
    B, S, D = x.shape
    depth = w_in.shape[0]
    T = B * S
    assert S % (DIL_BLOCK * DIL_PAIRS[-1][1]) == 0 and D % LANES == 0
    tm = _pick(S, (512, 256, 128))
    t_diff = _pick(S, (512, 256))
    tq_fox = _pick(S, (1024, 512, 256))
    tkd_fox = min(tq_fox, 512)
    tc = _pick(S, (128, 64))
    tt_sel = _pick(T, (512, 256))
    tt_peer = _pick(T, (512, 256))
    te = 1024

    n_sat = _bucket_saturation()
    n_near = min(-(-(n_sat + t_diff - 1) // t_diff), S // t_diff)
    rel_bias = rel_bias.astype(F32)
    dil_bias = [_dil_bias(rel_bias[:, :N_HEADS], w, r) for w, r in DIL_PAIRS]
    diff_bias = _diff_bias(rel_bias[:, N_HEADS:], t_diff, n_near) * LOG2E
    far = rel_bias[REL_BUCKETS - 1, N_HEADS:] * LOG2E
    far_hi = far.astype(BF16).astype(F32)
    lane = jnp.arange(LANES)
    aug = jnp.zeros((2 + N_HEADS, LANES), F32)
    aug = aug.at[0].set(jnp.where(lane == HEAD_DIM, 1.0, 0.0))
    aug = aug.at[1].set(jnp.where((lane == HEAD_DIM) | (lane == HEAD_DIM + 1), 1.0, 0.0))
    aug = aug.at[2:].set(jnp.where(lane[None] == HEAD_DIM, far_hi[:, None],
                                   jnp.where(lane[None] == HEAD_DIM + 1, (far - far_hi)[:, None], 0.0)))
    ones64, ones32 = _block_ones(HEAD_DIM), _block_ones(DIFF_HALF)
    tri = jnp.asarray(np.tril(np.ones((tm, tm), np.float32)), BF16)

    sizes = [GROUP, GROUP, GROUP, N_HEADS] + [GROUP] * 7
    starts = np.cumsum([0] + sizes)
    order = [0, 1, 2, 4, 5, 6, 7, 8, 9, 10]

    x = x.astype(F32)
    for layer in range(depth):
        wl = w_in[layer]
        w_packed = jnp.concatenate(
            [wl[:, starts[k]:starts[k + 1]] for k in order]
            + [wl[:, starts[3]:starts[4]], jnp.zeros((D, LANES - N_HEADS), wl.dtype)], axis=1).astype(BF16)
        bf = jnp.zeros((1, LANES), F32).at[0, :N_HEADS].set(b_forget[layer].astype(F32))
        gains = jnp.stack([
            jnp.tile(fox_q_norm[layer].astype(F32), N_HEADS) * (HEAD_DIM ** -0.5 * LOG2E),
            jnp.tile(fox_k_norm[layer].astype(F32), N_HEADS),
            jnp.tile(dil_q_norm[layer].astype(F32), N_HEADS) * HEAD_DIM ** -0.5,
            jnp.tile(dil_k_norm[layer].astype(F32), N_HEADS),
            jnp.tile(diff_q_norm[layer].astype(F32), 2 * N_HEADS) * (DIFF_HALF ** -0.5 * LOG2E),
            jnp.tile(diff_k_norm[layer].astype(F32), 2 * N_HEADS),
            jnp.zeros((GROUP,), F32), jnp.zeros((GROUP,), F32)])
        (fq, fk, fv, cq1, cq2, ck, cv, dq, dk, dv, su) = _inproj(
            x, norm_mix[layer].astype(F32)[None], w_packed, bf, gains, ones64, ones32, tri, aug, tm)

        fox_o = _fox_attention(fq, fk, fv, tq_fox, tq_fox, tkd_fox)

        lam_init = 0.8 - 0.6 * math.exp(-0.3 * layer)
        lam_p = diff_lambda[layer].astype(F32)
        lam = jnp.exp(jnp.sum(lam_p[0] * lam_p[1])) - jnp.exp(jnp.sum(lam_p[2] * lam_p[3])) + lam_init
        out_gain = jnp.zeros((1, LANES), F32).at[0, :HEAD_DIM].set(diff_out_norm[layer].astype(F32))
        diff_o = _diff_attention(jnp.stack([lam, 1.0 - lam_init]).astype(F32), cq1, cq2, ck, cv, diff_bias,
                                 out_gain, t_diff)

        branches = [_dil_branch(dq, dk, dv, dil_bias[n], r) for n, (_, r) in enumerate(DIL_PAIRS)]
        dil_o = _dil_combine([b[0] for b in branches], [b[1] for b in branches], tm)

        s5p = _s5_params(s5_a_re[layer], s5_a_im[layer], s5_log_dt[layer], s5_b_re[layer], s5_b_im[layer],
                         s5_c_re[layer], s5_c_im[layer], s5_d[layer])
        s5_o = _s5(su.reshape(S * B, GROUP), *s5p, s5_glu_w[layer].astype(BF16),
                   s5_glu_b[layer].astype(F32)[None], B, tc).reshape(S, B * GROUP)

        wo = w_out[layer]
        x = _outproj(x, fox_o, dil_o, diff_o, s5_o,
                     _pad_head_rows(wo[0:GROUP]), wo[GROUP:2 * GROUP].astype(BF16),
                     _pad_head_rows(wo[2 * GROUP:3 * GROUP]), wo[3 * GROUP:].astype(BF16), tm)

        x2 = x.reshape(T, D)
        h, p1, n1, p2, r2 = _peer_select(x2, norm_ffn[layer].astype(F32)[None], peer_wq[layer].astype(BF16),
                                         peer_keys[layer].astype(BF16), tt_sel)
        x = _peer_dense(x2, h, p1, n1, p2, r2, peer_u[layer].astype(BF16), peer_v[layer].astype(BF16).T,
                        tt_peer, te).reshape(B, S, D)
    return x
```

```python
import functools
import math

import numpy as np
import jax
import jax.numpy as jnp
from jax import lax
from jax.experimental import pallas as pl
from jax.experimental.pallas import tpu as pltpu

F32 = jnp.float32
BF16 = jnp.bfloat16

LANES = 128
GROUP = 256
HEAD_DIM = 64
N_HEADS = GROUP // HEAD_DIM
DIFF_HALF = HEAD_DIM // 2
S5_CH = 16
S5_GROUPS = GROUP // S5_CH
S5_STATE = 64
S5_WIDE = S5_GROUPS * S5_STATE
DIL_BLOCK = 128
DIL_PAIRS = ((128, 1), (512, 4), (2048, 16))
DIL_SUPER = DIL_BLOCK * DIL_PAIRS[-1][1]
REL_BUCKETS = 32
REL_MAX_EXACT = REL_BUCKETS // 2
REL_MAX_DIST = 2048
N_KEYS = 128
PEER_HEADS = 8
PEER_TOPK = 16
RMS_EPS = 1e-6
NEG = -1e30
LOG2E = math.log2(math.e)
GELU_A = -2.0 * math.sqrt(2.0 / math.pi) * LOG2E
GELU_B = GELU_A * 0.044715
PEER_SUB = 256
VMEM_LIMIT = 56 * 1024 * 1024

IN_COLS = 10 * GROUP + LANES
FORGET_COL = 10 * GROUP


def _cparams(sem):
    return pltpu.CompilerParams(dimension_semantics=sem, vmem_limit_bytes=VMEM_LIMIT)


def _dot(a, b):
    return jnp.dot(a, b, preferred_element_type=F32)


def _dot_nt(a, b):
    return lax.dot_general(a, b, (((1,), (1,)), ((), ())), preferred_element_type=F32)


def _split3(x):
    hi = x.astype(BF16)
    r = x - hi.astype(F32)
    mid = r.astype(BF16)
    lo = (r - mid.astype(F32)).astype(BF16)
    return hi, mid, lo


def _gelu(x):
    return 0.5 * x * (1.0 + jnp.tanh(math.sqrt(2.0 / math.pi) * (x + 0.044715 * (x * x * x))))


def _t5_bucket(dist):
    n = jnp.maximum(dist, 0)
    nf = jnp.maximum(n, REL_MAX_EXACT).astype(F32)
    large = REL_MAX_EXACT + (jnp.log(nf / REL_MAX_EXACT) / math.log(REL_MAX_DIST / REL_MAX_EXACT)
                             * (REL_BUCKETS - REL_MAX_EXACT)).astype(jnp.int32)
    large = jnp.minimum(large, REL_BUCKETS - 1)
    return jnp.where(n < REL_MAX_EXACT, n, large)


def _bucket_saturation():
    n = REL_MAX_EXACT * (REL_MAX_DIST / REL_MAX_EXACT) ** ((REL_BUCKETS - 1 - REL_MAX_EXACT)
                                                            / (REL_BUCKETS - REL_MAX_EXACT))
    return int(math.ceil(n)) + 2


def _inproj_kernel(x_ref, g_ref, w_ref, bf_ref, gains_ref, ones64_ref, ones32_ref, tri_ref, aug_ref,
                   fq_ref, fk_ref, fv_ref, cq1_ref, cq2_ref, ck_ref, cv_ref,
                   dq_ref, dk_ref, dv_ref, su_ref, carry_ref):
    @pl.when(pl.program_id(1) == 0)
    def _():
        carry_ref[...] = jnp.zeros_like(carry_ref)

    x = x_ref[0]
    tm = x.shape[0]
    h = x * lax.rsqrt(jnp.mean(x * x, axis=-1, keepdims=True) + RMS_EPS) * g_ref[...]
    z = _dot(h.astype(BF16), w_ref[...])

    def group(idx):
        return z[:, idx * GROUP:(idx + 1) * GROUP]

    def head_norm(zg, ones_ref, width, gain_row):
        ss = _dot((zg * zg).astype(BF16), ones_ref[...])
        return zg * lax.rsqrt(ss * (1.0 / width) + RMS_EPS) * gains_ref[gain_row:gain_row + 1, :]

    lane = lax.broadcasted_iota(jnp.int32, (tm, LANES), 1)

    def head_chunk(zg, hh):
        c = zg[:, (hh // 2) * LANES:(hh // 2 + 1) * LANES]
        if hh % 2:
            c = pltpu.roll(c, HEAD_DIM, axis=1)
        return c

    f = z[:, FORGET_COL:FORGET_COL + LANES] + bf_ref[...]
    ls = jnp.minimum(f, 0.0) - jnp.log1p(jnp.exp(-jnp.abs(f)))
    p0, p1, p2 = _split3(ls)
    tri = tri_ref[...]
    cum = (_dot(tri, p0) + _dot(tri, p1)) + _dot(tri, p2) + carry_ref[...]
    carry_ref[...] = cum[tm - 1:tm, :]

    fq = head_norm(group(0), ones64_ref, HEAD_DIM, 0)
    fk = head_norm(group(1), ones64_ref, HEAD_DIM, 1)
    fv = group(2)
    dq = head_norm(group(3), ones64_ref, HEAD_DIM, 2)
    dk = head_norm(group(4), ones64_ref, HEAD_DIM, 3)
    cq = head_norm(group(6), ones32_ref, DIFF_HALF, 4)
    ck = head_norm(group(7), ones32_ref, DIFF_HALF, 5)
    cv = group(8)

    for hf in range(2):
        dq_ref[0, hf] = dq[:, hf * LANES:(hf + 1) * LANES]
        dk_ref[0, hf] = dk[:, hf * LANES:(hf + 1) * LANES]
        dv_ref[0, hf] = group(5)[:, hf * LANES:(hf + 1) * LANES]
    su_ref[...] = group(9)

    one = jnp.ones((tm, LANES), F32)
    zero = jnp.zeros((tm, LANES), F32)
    for hh in range(N_HEADS):
        c = jnp.broadcast_to(cum[:, hh:hh + 1], (tm, LANES)) * LOG2E
        c0, c1, c2 = _split3(c)
        c0, c1, c2 = c0.astype(F32), c1.astype(F32), c2.astype(F32)
        qa = jnp.where(lane == 64, c0, jnp.where(lane == 65, c1, jnp.where(lane == 66, c2,
             jnp.where(lane < 70, one, zero))))
        ka = jnp.where(lane == 67, -c0, jnp.where(lane == 68, -c1, jnp.where(lane == 69, -c2,
             jnp.where(lane < 70, one, zero))))
        fq_ref[0, hh] = jnp.where(lane < HEAD_DIM, head_chunk(fq, hh), qa).astype(BF16)
        fk_ref[0, hh] = jnp.where(lane < HEAD_DIM, head_chunk(fk, hh), ka).astype(BF16)
        vaug = aug_ref[0:1, :]
        fv_ref[0, hh] = jnp.where(lane < HEAD_DIM, head_chunk(fv, hh), vaug).astype(BF16)
        qaug = aug_ref[1:2, :]
        cqc = head_chunk(cq, hh)
        cq1_ref[0, hh] = jnp.where(lane < DIFF_HALF, cqc, qaug).astype(BF16)
        cq2_ref[0, hh] = jnp.where((lane >= DIFF_HALF) & (lane < HEAD_DIM), cqc, qaug).astype(BF16)
        kaug = aug_ref[2 + hh:3 + hh, :]
        ck_ref[0, hh] = jnp.where(lane < HEAD_DIM, head_chunk(ck, hh), kaug).astype(BF16)
        cv_ref[0, hh] = jnp.where(lane < HEAD_DIM, head_chunk(cv, hh), vaug).astype(BF16)


def _inproj(x, g, w, bf, gains, ones64, ones32, tri, aug, tm):
    B, S, D = x.shape
    nt = S // tm
    hm = jax.ShapeDtypeStruct((B, N_HEADS, S, LANES), BF16)
    nat = jax.ShapeDtypeStruct((B, 2, S, LANES), F32)
    hm_spec = pl.BlockSpec((1, N_HEADS, tm, LANES), lambda b, i: (b, 0, i, 0))
    nat_spec = pl.BlockSpec((1, 2, tm, LANES), lambda b, i: (b, 0, i, 0))

    def full(a):
        return pl.BlockSpec(a.shape, lambda b, i: (0,) * a.ndim)

    return pl.pallas_call(
        _inproj_kernel,
        grid=(B, nt),
        in_specs=[pl.BlockSpec((1, tm, D), lambda b, i: (b, i, 0)),
                  full(g), full(w), full(bf), full(gains), full(ones64), full(ones32), full(tri), full(aug)],
        out_specs=[hm_spec] * 7 + [nat_spec] * 3 + [pl.BlockSpec((tm, GROUP), lambda b, i: (i, b))],
        out_shape=[hm] * 7 + [nat] * 3 + [jax.ShapeDtypeStruct((S, B * GROUP), F32)],
        scratch_shapes=[pltpu.VMEM((1, LANES), F32)],
        compiler_params=_cparams(("arbitrary", "arbitrary")),
        name="inproj",
    )(x, g, w, bf, gains, ones64, ones32, tri, aug)


def _flash_step(q, k, v, m, acc, bias=None):
    s = _dot_nt(q, k)
    if bias is not None:
        s = s + bias
    m_new = jnp.maximum(m, jnp.max(s, axis=1, keepdims=True))
    alpha = jnp.exp2(m - m_new)
    p = jnp.exp2(s - m_new)
    acc = alpha * acc + _dot(p.astype(BF16), v)
    return m_new, acc


def _fox_kernel(q_ref, k_ref, v_ref, o_ref, *, tq, tk, tkd):
    i = pl.program_id(2)
    q = q_ref[0, 0]

    def kv(off, size):
        return k_ref[0, 0, pl.ds(off, size), :], v_ref[0, 0, pl.ds(off, size), :]

    def body(j, carry):
        m, acc = carry
        k, v = kv(pl.multiple_of(j * tk, tk), tk)
        return _flash_step(q, k, v, m, acc)

    m0 = jnp.full((tq, 1), NEG, F32)
    acc0 = jnp.zeros((tq, LANES), F32)
    m, acc = lax.fori_loop(0, i * (tq // tk), body, (m0, acc0))
    for d in range(tq // tkd):
        rows = tq - d * tkd
        k, v = kv(pl.multiple_of(i * tq + d * tkd, tkd), tkd)
        row = lax.broadcasted_iota(jnp.int32, (rows, tkd), 0)
        col = lax.broadcasted_iota(jnp.int32, (rows, tkd), 1)
        mask = jnp.where(col <= row, 0.0, NEG)
        m_d, acc_d = _flash_step(q[d * tkd:], k, v, m[d * tkd:], acc[d * tkd:], mask)
        if d:
            m = jnp.concatenate([m[:d * tkd], m_d], axis=0)
            acc = jnp.concatenate([acc[:d * tkd], acc_d], axis=0)
        else:
            m, acc = m_d, acc_d
    den = jnp.broadcast_to(acc[:, HEAD_DIM:HEAD_DIM + 1], (tq, LANES))
    o_ref[0, 0] = (acc / den).astype(BF16)


def _fox_attention(q, k, v, tq, tk, tkd):
    B, H, S, _ = q.shape
    return pl.pallas_call(
        functools.partial(_fox_kernel, tq=tq, tk=tk, tkd=tkd),
        grid=(B, H, S // tq),
        in_specs=[pl.BlockSpec((1, 1, tq, LANES), lambda b, h, i: (b, h, i, 0)),
                  pl.BlockSpec((1, 1, S, LANES), lambda b, h, i: (b, h, 0, 0)),
                  pl.BlockSpec((1, 1, S, LANES), lambda b, h, i: (b, h, 0, 0))],
        out_specs=pl.BlockSpec((1, 1, tq, LANES), lambda b, h, i: (b, h, i, 0)),
        out_shape=jax.ShapeDtypeStruct((B, H, S, LANES), BF16),
        compiler_params=_cparams(("arbitrary", "arbitrary", "arbitrary")),
        name="fox_attention",
    )(q, k, v)


def _diff_kernel(lam_ref, q1_ref, q2_ref, k_ref, v_ref, bias_ref, gain_ref, o_ref, *, t, n_near):
    i = pl.program_id(2)
    q = jnp.concatenate([q1_ref[0, 0], q2_ref[0, 0]], axis=0)
    n_pair = jnp.maximum(i - (n_near - 1), 0) // 2

    def kv(j, size):
        off = pl.multiple_of(j * size, size)
        return k_ref[0, 0, pl.ds(off, size), :], v_ref[0, 0, pl.ds(off, size), :]

    def far(j, carry):
        m, acc = carry
        k, v = kv(j, 2 * t)
        return _flash_step(q, k, v, m, acc)

    def near(j, carry):
        m, acc = carry
        k, v = kv(j, t)
        b = bias_ref[0, jnp.minimum(i - j, n_near)]
        return _flash_step(q, k, v, m, acc, jnp.concatenate([b, b], axis=0))

    m0 = jnp.full((2 * t, 1), NEG, F32)
    a0 = jnp.zeros((2 * t, LANES), F32)
    carry = lax.fori_loop(0, n_pair, far, (m0, a0))
    _, acc = lax.fori_loop(2 * n_pair, i + 1, near, carry)
    a1, a2 = acc[:t], acc[t:]
    d1 = jnp.broadcast_to(a1[:, HEAD_DIM:HEAD_DIM + 1], (t, LANES))
    d2 = jnp.broadcast_to(a2[:, HEAD_DIM:HEAD_DIM + 1], (t, LANES))
    lane = lax.broadcasted_iota(jnp.int32, (t, LANES), 1)
    o = jnp.where(lane < HEAD_DIM, a1 / d1 - lam_ref[0] * (a2 / d2), 0.0)
    ms = jnp.sum(o * o, axis=1, keepdims=True) * (1.0 / HEAD_DIM)
    o_ref[0, 0] = (o * lax.rsqrt(ms + RMS_EPS) * gain_ref[...] * lam_ref[1]).astype(BF16)


def _diff_attention(lam, q1, q2, k, v, bias, gain, t):
    B, H, S, _ = q1.shape
    n_near = bias.shape[1] - 1
    qspec = pl.BlockSpec((1, 1, t, LANES), lambda b, h, i: (b, h, i, 0))
    kspec = pl.BlockSpec((1, 1, S, LANES), lambda b, h, i: (b, h, 0, 0))
    return pl.pallas_call(
        functools.partial(_diff_kernel, t=t, n_near=n_near),
        grid=(B, H, S // t),
        in_specs=[pl.BlockSpec(memory_space=pltpu.SMEM), qspec, qspec, kspec, kspec,
                  pl.BlockSpec((1, n_near + 1, t, t), lambda b, h, i: (h, 0, 0, 0)),
                  pl.BlockSpec((1, LANES), lambda b, h, i: (0, 0))],
        out_specs=qspec,
        out_shape=jax.ShapeDtypeStruct((B, H, S, LANES), BF16),
        compiler_params=_cparams(("arbitrary", "arbitrary", "arbitrary")),
        name="diff_attention",
    )(lam, q1, q2, k, v, bias, gain)


def _dil_kernel(qc_ref, kp_ref, kc_ref, vp_ref, vc_ref, b0_ref, b1_ref, b2_ref, o_ref, num_ref, st_ref):
    first = pl.program_id(1) == 0
    w = DIL_BLOCK
    lane = lax.broadcasted_iota(jnp.int32, (w, GROUP), 1)
    lane_s = lax.broadcasted_iota(jnp.int32, (w, LANES), 1)
    heads = [(lane >= hh * HEAD_DIM) & (lane < (hh + 1) * HEAD_DIM) for hh in range(N_HEADS)]
    ones = jnp.ones((w, LANES), BF16)

    def rows(cur_ref, prev_ref, off, res, dil):
        ref, base = (cur_ref, off) if off >= 0 else (prev_ref, off + DIL_SUPER)
        if dil == 1:
            parts = [ref[0, hf, base:base + w, :] for hf in range(2)]
        else:
            parts = [ref[0, hf, pl.ds(base + res, w, stride=dil), :] for hf in range(2)]
        return jnp.concatenate(parts, axis=1).astype(BF16)

    for n, (bias_ref, (_, dil)) in enumerate(zip((b0_ref, b1_ref, b2_ref), DIL_PAIRS)):
        span = w * dil
        for u in range(DIL_SUPER // span):
            off = u * span
            for res in range(dil):
                q = rows(qc_ref, qc_ref, off, res, dil)
                kc, vc = rows(kc_ref, kp_ref, off, res, dil), rows(vc_ref, vp_ref, off, res, dil)
                kp, vp = rows(kc_ref, kp_ref, off - span, res, dil), rows(vc_ref, vp_ref, off - span, res, dil)
                num = jnp.zeros((w, GROUP), F32)
                st = jnp.zeros((w, LANES), F32)
                for hh in range(N_HEADS):
                    qh = jnp.where(heads[hh], q, jnp.zeros_like(q))
                    sp = _dot_nt(qh, kp) + bias_ref[hh, :, 0:w]
                    if u == 0:
                        sp = sp + jnp.where(first, NEG, 0.0)
                    sc = _dot_nt(qh, kc) + bias_ref[hh, :, w:2 * w]
                    m = jnp.maximum(jnp.max(sp, axis=1, keepdims=True), jnp.max(sc, axis=1, keepdims=True))
                    pp = jnp.exp(sp - m).astype(BF16)
                    pc = jnp.exp(sc - m).astype(BF16)
                    den = _dot(pp, ones) + _dot(pc, ones)
                    o = _dot(pp, vp) + _dot(pc, vc)
                    num = jnp.where(heads[hh], o, num)
                    st = jnp.where(lane_s == hh, den, jnp.where(lane_s == N_HEADS + hh, m, st))
                dst = slice(off, off + w) if dil == 1 else pl.ds(off + res, w, stride=dil)
                for hf in range(2):
                    num_ref[n, hf, dst, :] = num[:, hf * LANES:(hf + 1) * LANES]
                st_ref[n, dst, :] = st

    tm = 2 * w
    lane2 = lax.broadcasted_iota(jnp.int32, (tm, GROUP), 1)
    for c in range(DIL_SUPER // tm):
        sl = slice(c * tm, (c + 1) * tm)
        nums = [jnp.concatenate([num_ref[n, 0, sl, :], num_ref[n, 1, sl, :]], axis=1) for n in range(3)]
        sts = [st_ref[n, sl, :] for n in range(3)]
        out = jnp.zeros((tm, GROUP), F32)
        for hh in range(N_HEADS):
            ms = [s[:, N_HEADS + hh:N_HEADS + hh + 1] for s in sts]
            ds = [s[:, hh:hh + 1] for s in sts]
            mx = jnp.maximum(jnp.maximum(ms[0], ms[1]), ms[2])
            ws = [jnp.exp(m - mx) for m in ms]
            den = ws[0] * ds[0] + ws[1] * ds[1] + ws[2] * ds[2]
            o = (ws[0] * nums[0] + ws[1] * nums[1] + ws[2] * nums[2]) / den
            out = jnp.where((lane2 >= hh * HEAD_DIM) & (lane2 < (hh + 1) * HEAD_DIM), o, out)
        o_ref[0, sl, :] = out.astype(BF16)


def _dilated_attention(q, k, v, biases):
    B, _, S, _ = q.shape
    cur = pl.BlockSpec((1, 2, DIL_SUPER, LANES), lambda b, n: (b, 0, n, 0))
    prev = pl.BlockSpec((1, 2, DIL_SUPER, LANES), lambda b, n: (b, 0, jnp.maximum(n - 1, 0), 0))
    bspec = pl.BlockSpec(biases[0].shape, lambda b, n: (0, 0, 0))
    return pl.pallas_call(
        _dil_kernel,
        grid=(B, S // DIL_SUPER),
        in_specs=[cur, prev, cur, prev, cur, bspec, bspec, bspec],
        out_specs=pl.BlockSpec((1, DIL_SUPER, GROUP), lambda b, n: (b, n, 0)),
        out_shape=jax.ShapeDtypeStruct((B, S, GROUP), BF16),
        scratch_shapes=[pltpu.VMEM((3, 2, DIL_SUPER, LANES), F32), pltpu.VMEM((3, DIL_SUPER, LANES), F32)],
        compiler_params=_cparams(("arbitrary", "arbitrary")),
        name="dilated_attention",
    )(q, k, k, v, v, *biases)


def _s5_kernel(u_ref, are_ref, aim_ref, bre_ref, bim_ref, cre_ref, cim_ref, d_ref, gw_ref, gb_ref,
               o_ref, xr_ref, xi_ref, bur_ref, bui_ref, sr_ref, si_ref, *, nb, tc):
    @pl.when(pl.program_id(0) == 0)
    def _():
        sr_ref[...] = jnp.zeros_like(sr_ref)
        si_ref[...] = jnp.zeros_like(si_ref)

    u = u_ref[...]
    ub = u.astype(BF16)
    bur_ref[...] = _dot(ub, bre_ref[...])
    bui_ref[...] = _dot(ub, bim_ref[...])
    ar = jnp.broadcast_to(are_ref[...], (nb, S5_WIDE))
    ai = jnp.broadcast_to(aim_ref[...], (nb, S5_WIDE))

    def step(t, carry):
        xr, xi = carry
        rows = pl.ds(pl.multiple_of(t * nb, nb), nb)
        nr = ar * xr - ai * xi + bur_ref[rows, :]
        ni = ar * xi + ai * xr + bui_ref[rows, :]
        xr_ref[rows, :] = nr
        xi_ref[rows, :] = ni
        return nr, ni

    xr, xi = lax.fori_loop(0, tc, step, (sr_ref[...], si_ref[...]))
    sr_ref[...] = xr
    si_ref[...] = xi
    y = (_dot(xr_ref[...].astype(BF16), cre_ref[...]) - _dot(xi_ref[...].astype(BF16), cim_ref[...])
         + d_ref[...] * u)
    g = _gelu(y)
    gate = _dot(g.astype(BF16), gw_ref[...]) + gb_ref[...]
    o_ref[...] = (g * (1.0 / (1.0 + jnp.exp(-gate)))).astype(BF16)


def _s5(u2, are, aim, bre, bim, cre, cim, d, gw, gb, nb, tc):
    rows = u2.shape[0]
    tr = tc * nb

    def full(a):
        return pl.BlockSpec(a.shape, lambda i: (0,) * a.ndim)

    consts = (are, aim, bre, bim, cre, cim, d, gw, gb)
    return pl.pallas_call(
        functools.partial(_s5_kernel, nb=nb, tc=tc),
        grid=(rows // tr,),
        in_specs=[pl.BlockSpec((tr, GROUP), lambda i: (i, 0))] + [full(a) for a in consts],
        out_specs=pl.BlockSpec((tr, GROUP), lambda i: (i, 0)),
        out_shape=jax.ShapeDtypeStruct((rows, GROUP), BF16),
        scratch_shapes=[pltpu.VMEM((tr, S5_WIDE), F32)] * 4 + [pltpu.VMEM((nb, S5_WIDE), F32)] * 2,
        compiler_params=_cparams(("arbitrary",)),
        name="s5_mixer",
    )(u2, *consts)


def _outproj_kernel(x_ref, fox_ref, dil_ref, dif_ref, s5_ref, wf_ref, wd_ref, wc_ref, ws_ref, o_ref):
    acc = x_ref[0] + _dot(dil_ref[0], wd_ref[...]) + _dot(s5_ref[...], ws_ref[...])
    for hh in range(N_HEADS):
        acc = acc + _dot(fox_ref[0, hh], wf_ref[hh]) + _dot(dif_ref[0, hh], wc_ref[hh])
    o_ref[0] = acc


def _outproj(x, fox, dil, dif, s5, wf, wd, wc, ws, tm):
    B, S, D = x.shape
    xspec = pl.BlockSpec((1, tm, D), lambda b, i: (b, i, 0))
    hspec = pl.BlockSpec((1, N_HEADS, tm, LANES), lambda b, i: (b, 0, i, 0))

    def full(a):
        return pl.BlockSpec(a.shape, lambda b, i: (0,) * a.ndim)

    return pl.pallas_call(
        _outproj_kernel,
        grid=(B, S // tm),
        in_specs=[xspec, hspec, pl.BlockSpec((1, tm, GROUP), lambda b, i: (b, i, 0)), hspec,
                  pl.BlockSpec((tm, GROUP), lambda b, i: (i, b)),
                  full(wf), full(wd), full(wc), full(ws)],
        out_specs=xspec,
        out_shape=jax.ShapeDtypeStruct((B, S, D), F32),
        compiler_params=_cparams(("arbitrary", "arbitrary")),
        name="outproj",
    )(x, fox, dil, dif, s5, wf, wd, wc, ws)


def _staircase(n):
    return [(a, b) for a in range(n) for b in range(n) if (a + 1) * (b + 1) <= n]


def _peer_select_kernel(x_ref, g_ref, wq_ref, keys_ref, h_ref, p1_ref, n1_ref, p2_ref, r2_ref):
    x = x_ref[...]
    tt = x.shape[0]
    h = (x * lax.rsqrt(jnp.mean(x * x, axis=-1, keepdims=True) + RMS_EPS) * g_ref[...]).astype(BF16)
    h_ref[...] = h
    q = _dot(h, wq_ref[...]).astype(BF16)
    n_top = PEER_TOPK + 1
    pairs = _staircase(n_top)
    for hh in range(PEER_HEADS):
        tops = []
        scores = []
        rank = jnp.full((N_KEYS, tt), float(N_KEYS), F32)
        for half in range(2):
            c = (hh * 2 + half) * N_KEYS
            s = _dot_nt(keys_ref[hh, half], q[:, c:c + N_KEYS])
            scores.append(s)
            vals = []
            for kk in range(n_top):
                mx = jnp.max(s, axis=0, keepdims=True)
                vals.append(mx)
                hit = s == mx
                if half == 1:
                    rank = jnp.where(hit, float(kk), rank)
                s = jnp.where(hit, NEG, s)
            tops.append(vals)
        cand = jnp.concatenate([tops[0][a] + tops[1][b] for a, b in pairs]
                               + [jnp.full((8 - len(pairs) % 8, tt), NEG, F32)], axis=0)
        best = []
        for _ in range(n_top):
            mx = jnp.max(cand, axis=0, keepdims=True)
            best.append(mx)
            cand = jnp.where(cand == mx, NEG, cand)
        z = jnp.zeros((1, tt), F32)
        for kk in range(PEER_TOPK):
            z = z + jnp.exp(best[kk] - best[0])
        cut = 0.5 * (best[PEER_TOPK - 1] + best[PEER_TOPK])
        n1 = jnp.zeros((N_KEYS, tt), F32)
        for b in range(PEER_TOPK):
            n1 = n1 + jnp.where(scores[0] >= cut - tops[1][b], 1.0, 0.0)
        p1_ref[hh] = jnp.exp(scores[0] - tops[0][0]) * (1.0 / z)
        n1_ref[hh] = n1
        p2_ref[hh] = pltpu.bitcast(jnp.exp(scores[1] - tops[1][0]).astype(BF16), jnp.uint32)
        r2_ref[hh] = pltpu.bitcast(rank.astype(BF16), jnp.uint32)


def _peer_select(x2, g, wq, keys, tt):
    T, D = x2.shape
    rows = pl.BlockSpec((PEER_HEADS, N_KEYS, tt), lambda i: (0, 0, i))
    words = pl.BlockSpec((PEER_HEADS, N_KEYS // 2, tt), lambda i: (0, 0, i))
    return pl.pallas_call(
        _peer_select_kernel,
        grid=(T // tt,),
        in_specs=[pl.BlockSpec((tt, D), lambda i: (i, 0)),
                  pl.BlockSpec(g.shape, lambda i: (0, 0)),
                  pl.BlockSpec(wq.shape, lambda i: (0, 0)),
                  pl.BlockSpec(keys.shape, lambda i: (0, 0, 0, 0))],
        out_specs=[pl.BlockSpec((tt, D), lambda i: (i, 0)), rows, rows, words, words],
        out_shape=[jax.ShapeDtypeStruct((T, D), BF16),
                   jax.ShapeDtypeStruct((PEER_HEADS, N_KEYS, T), F32),
                   jax.ShapeDtypeStruct((PEER_HEADS, N_KEYS, T), F32),
                   jax.ShapeDtypeStruct((PEER_HEADS, N_KEYS // 2, T), jnp.uint32),
                   jax.ShapeDtypeStruct((PEER_HEADS, N_KEYS // 2, T), jnp.uint32)],
        compiler_params=_cparams(("arbitrary",)),
        name="peer_select",
    )(x2, g, wq, keys)


def _gelu_gate(x, g):
    e = jnp.exp2(x * (GELU_A + GELU_B * (x * x)))
    return x * g / (1.0 + e)


def _peer_dense_kernel(x_ref, h_ref, p1_ref, n1_ref, p2_ref, r2_ref, u_ref, vt_ref, o_ref,
                       a_ref, w_ref, acc_ref, *, te, tt, n_e):
    s = pl.program_id(0)

    @pl.when(s == 0)
    def _():
        a_ref[...] = jnp.zeros_like(a_ref)
        w_ref[...] = jnp.zeros_like(w_ref)
        acc_ref[...] = jnp.zeros_like(acc_ref)

    slot_a = s % 2
    slot_b = 1 - slot_a
    j_b = jnp.maximum(s - 1, 0) % n_e
    j_c = jnp.maximum(s - 2, 0) % n_e

    h = h_ref[...]
    for q in range(te // PEER_SUB):
        a_ref[slot_a, q * PEER_SUB:(q + 1) * PEER_SUB, :] = _dot_nt(u_ref[q * PEER_SUB:(q + 1) * PEER_SUB, :], h)

    acc_ref[...] = jnp.where(j_c == 0, 0.0, acc_ref[...]) + _dot(vt_ref[...], w_ref[slot_a])

    rows_per_step = te // N_KEYS
    rows = pl.ds(pl.multiple_of(j_b * rows_per_step, rows_per_step), rows_per_step)
    for r in range(rows_per_step):
        for c in range(tt // LANES):
            cols = slice(c * LANES, (c + 1) * LANES)
            g = jnp.zeros((N_KEYS, LANES), BF16)
            for hh in range(PEER_HEADS):
                gate = jnp.broadcast_to(p1_ref[hh, rows, cols][r:r + 1], (N_KEYS, LANES)).astype(BF16)
                count = jnp.broadcast_to(n1_ref[hh, rows, cols][r:r + 1], (N_KEYS, LANES)).astype(BF16)
                r2 = pltpu.bitcast(r2_ref[hh, :, cols], BF16)
                p2 = pltpu.bitcast(p2_ref[hh, :, cols], BF16)
                sel = jnp.where(r2 < count, p2, jnp.zeros((), BF16))
                g = g + sel * gate
            a = a_ref[slot_b, r * N_KEYS:(r + 1) * N_KEYS, cols]
            w_ref[slot_b, r * N_KEYS:(r + 1) * N_KEYS, cols] = _gelu_gate(a, g.astype(F32)).astype(BF16)

    @pl.when((j_c == n_e - 1) & (s >= 2))
    def _():
        o_ref[...] = x_ref[...] + acc_ref[...].T


def _peer_dense(x2, h, p1, n1, p2, r2, u, vt, tt, te):
    T, D = x2.shape
    n_e = u.shape[0] // te
    last = (T // tt) * n_e - 1

    def tok(lag):
        return lambda s: jnp.clip(s - lag, 0, last) // n_e

    def tile(lag):
        return lambda s: jnp.clip(s - lag, 0, last) % n_e

    rows = pl.BlockSpec((PEER_HEADS, N_KEYS, tt), lambda s: (0, 0, tok(1)(s)))
    words = pl.BlockSpec((PEER_HEADS, N_KEYS // 2, tt), lambda s: (0, 0, tok(1)(s)))
    return pl.pallas_call(
        functools.partial(_peer_dense_kernel, te=te, tt=tt, n_e=n_e),
        grid=(last + 3,),
        in_specs=[pl.BlockSpec((tt, D), lambda s: (tok(2)(s), 0)),
                  pl.BlockSpec((tt, D), lambda s: (tok(0)(s), 0)),
                  rows, rows, words, words,
                  pl.BlockSpec((te, D), lambda s: (tile(0)(s), 0)),
                  pl.BlockSpec((D, te), lambda s: (0, tile(2)(s)))],
        out_specs=pl.BlockSpec((tt, D), lambda s: (tok(2)(s), 0)),
        out_shape=jax.ShapeDtypeStruct((T, D), F32),
        scratch_shapes=[pltpu.VMEM((2, te, tt), F32), pltpu.VMEM((2, te, tt), BF16), pltpu.VMEM((D, tt), F32)],
        compiler_params=_cparams(("arbitrary",)),
        name="peer_dense",
    )(x2, h, p1, n1, p2, r2, u, vt)


def _block_ones(width):
    idx = np.arange(GROUP) // width
    return jnp.asarray(idx[:, None] == idx[None, :], BF16)


def _pad_head_rows(w):
    w = w.reshape(N_HEADS, HEAD_DIM, -1)
    return jnp.concatenate([w, jnp.zeros_like(w)], axis=1).astype(BF16)


def _dil_bias(rel_bias, window, dil):
    ws = window // dil
    j = jnp.arange(DIL_BLOCK)[:, None]
    m = jnp.arange(2 * DIL_BLOCK)[None, :]
    sub = DIL_BLOCK + j - m
    band = (sub >= 0) & (sub <= ws)
    bias = rel_bias[_t5_bucket(dil * jnp.clip(sub, 0, ws))].transpose(2, 0, 1).astype(F32)
    return jnp.where(band[None], bias, NEG)


def _diff_bias(rel_bias, t, n_near):
    dist = jnp.arange(-(t - 1), n_near * t + 2)
    b = rel_bias[_t5_bucket(dist)].astype(F32) - rel_bias[REL_BUCKETS - 1].astype(F32)
    b = jnp.where((dist >= 0)[:, None], b, NEG).T
    rev = b[:, ::-1]
    n = b.shape[1]
    tiles = []
    for delta in range(n_near):
        start = n - 1 - (delta * t + 2 * t - 2)
        u = jnp.pad(rev[:, start:start + 2 * t - 1], ((0, 0), (0, 1)))
        flat = jnp.tile(u, (1, t))[:, t - 1:t - 1 + t * (2 * t - 1)]
        tiles.append(flat.reshape(-1, t, 2 * t - 1)[:, :, :t])
    tiles.append(jnp.zeros_like(tiles[0]))
    return jnp.stack(tiles, axis=1)


def _s5_params(a_re, a_im, log_dt, b_re, b_im, c_re, c_im, d_skip):
    G, N, C = S5_GROUPS, S5_STATE, S5_CH
    dt = jnp.exp(log_dt.astype(F32))[:, None]
    ar, ai = a_re.astype(F32), a_im.astype(F32)
    mag = jnp.exp(dt * ar)
    abar_re, abar_im = mag * jnp.cos(dt * ai), mag * jnp.sin(dt * ai)
    den = ar * ar + ai * ai
    nr, ni = abar_re - 1.0, abar_im
    coef_re = (nr * ar + ni * ai) / den
    coef_im = (ni * ar - nr * ai) / den
    br, bi = b_re.astype(F32), b_im.astype(F32)
    bbar_re = coef_re[..., None] * br - coef_im[..., None] * bi
    bbar_im = coef_re[..., None] * bi + coef_im[..., None] * br
    eye = jnp.eye(G, dtype=F32)

    def b_blockdiag(bb):
        return jnp.einsum('gnc,gk->gckn', bb, eye).reshape(G * C, G * N).astype(BF16)

    def c_blockdiag(cc):
        return jnp.einsum('gcn,gk->gnkc', cc.astype(F32), eye).reshape(G * N, G * C).astype(BF16)

    return (abar_re.reshape(1, G * N), abar_im.reshape(1, G * N), b_blockdiag(bbar_re), b_blockdiag(bbar_im),
            c_blockdiag(c_re), c_blockdiag(c_im), d_skip.astype(F32).reshape(1, G * C))


def _pick(n, pref):
    for c in pref:
        if n % c == 0:
            return c
    raise ValueError("unsupported size %d" % n)


def kernel(x, w_in, b_forget, w_out, norm_mix, norm_ffn, fox_q_norm, fox_k_norm, dil_q_norm, dil_k_norm, diff_q_norm, diff_k_norm, diff_lambda, diff_out_norm, rel_bias, s5_a_re, s5_a_im, s5_log_dt, s5_b_re, s5_b_im, s5_c_re, s5_c_im, s5_d, s5_glu_w, s5_glu_b, peer_wq, peer_keys, peer_u, peer_v):
    B, S, D = x.shape
    depth = w_in.shape[0]
    T = B * S
    assert S % (DIL_BLOCK * DIL_PAIRS[-1][1]) == 0 and D % LANES == 0
    tm = _pick(S, (512, 256, 128))
    t_diff = _pick(S, (512, 256))
    tq_fox = _pick(S, (1024, 512, 256))
    tkd_fox = min(tq_fox, 512)
    tc = _pick(S, (128, 64))
    tt_sel = _pick(T, (512, 256))
    tt_peer = _pick(T, (512, 256))
    te = 1024

    n_sat = _bucket_saturation()
    n_near = min(-(-(n_sat + t_diff - 1) // t_diff), S // t_diff)
    rel_bias = rel_bias.astype(F32)
    dil_bias = [_dil_bias(rel_bias[:, :N_HEADS], w, r) for w, r in DIL_PAIRS]
    diff_bias = _diff_bias(rel_bias[:, N_HEADS:], t_diff, n_near) * LOG2E
    far = rel_bias[REL_BUCKETS - 1, N_HEADS:] * LOG2E
    far_hi = far.astype(BF16).astype(F32)
    lane = jnp.arange(LANES)
    aug = jnp.zeros((2 + N_HEADS, LANES), F32)
    aug = aug.at[0].set(jnp.where(lane == HEAD_DIM, 1.0, 0.0))
    aug = aug.at[1].set(jnp.where((lane == HEAD_DIM) | (lane == HEAD_DIM + 1), 1.0, 0.0))
    aug = aug.at[2:].set(jnp.where(lane[None] == HEAD_DIM, far_hi[:, None],
                                   jnp.where(lane[None] == HEAD_DIM + 1, (far - far_hi)[:, None], 0.0)))
    ones64, ones32 = _block_ones(HEAD_DIM), _block_ones(DIFF_HALF)
    tri = jnp.asarray(np.tril(np.ones((tm, tm), np.float32)), BF16)

    sizes = [GROUP, GROUP, GROUP, N_HEADS] + [GROUP] * 7
    starts = np.cumsum([0] + sizes)
    order = [0, 1, 2, 4, 5, 6, 7, 8, 9, 10]

    x = x.astype(F32)
    for layer in range(depth):
        wl = w_in[layer]
        w_packed = jnp.concatenate(
            [wl[:, starts[k]:starts[k + 1]] for k in order]
            + [wl[:, starts[3]:starts[4]], jnp.zeros((D, LANES - N_HEADS), wl.dtype)], axis=1).astype(BF16)
        bf = jnp.zeros((1, LANES), F32).at[0, :N_HEADS].set(b_forget[layer].astype(F32))
        gains = jnp.stack([
            jnp.tile(fox_q_norm[layer].astype(F32), N_HEADS) * (HEAD_DIM ** -0.5 * LOG2E),
            jnp.tile(fox_k_norm[layer].astype(F32), N_HEADS),
            jnp.tile(dil_q_norm[layer].astype(F32), N_HEADS) * HEAD_DIM ** -0.5,
            jnp.tile(dil_k_norm[layer].astype(F32), N_HEADS),
            jnp.tile(diff_q_norm[layer].astype(F32), 2 * N_HEADS) * (DIFF_HALF ** -0.5 * LOG2E),
            jnp.tile(diff_k_norm[layer].astype(F32), 2 * N_HEADS),
            jnp.zeros((GROUP,), F32), jnp.zeros((GROUP,), F32)])
        (fq, fk, fv, cq1, cq2, ck, cv, dq, dk, dv, su) = _inproj(
            x, norm_mix[layer].astype(F32)[None], w_packed, bf, gains, ones64, ones32, tri, aug, tm)

        fox_o = _fox_attention(fq, fk, fv, tq_fox, tq_fox, tkd_fox)

        lam_init = 0.8 - 0.6 * math.exp(-0.3 * layer)
        lam_p = diff_lambda[layer].astype(F32)
        lam = jnp.exp(jnp.sum(lam_p[0] * lam_p[1])) - jnp.exp(jnp.sum(lam_p[2] * lam_p[3])) + lam_init
        out_gain = jnp.zeros((1, LANES), F32).at[0, :HEAD_DIM].set(diff_out_norm[layer].astype(F32))
        diff_o = _diff_attention(jnp.stack([lam, 1.0 - lam_init]).astype(F32), cq1, cq2, ck, cv, diff_bias,
                                 out_gain, t_diff)

        dil_o = _dilated_attention(dq, dk, dv, dil_bias)

        s5p = _s5_params(s5_a_re[layer], s5_a_im[layer], s5_log_dt[layer], s5_b_re[layer], s5_b_im[layer],
                         s5_c_re[layer], s5_c_im[layer], s5_d[layer])
        s5_o = _s5(su.reshape(S * B, GROUP), *s5p, s5_glu_w[layer].astype(BF16),
                   s5_glu_b[layer].astype(F32)[None], B, tc).reshape(S, B * GROUP)

        wo = w_out[layer]
        x = _outproj(x, fox_o, dil_o, diff_o, s5_o,
                     _pad_head_rows(wo[0:GROUP]), wo[GROUP:2 * GROUP].astype(BF16),
                     _pad_head_rows(wo[2 * GROUP:3 * GROUP]), wo[3 * GROUP:].astype(BF16), tm)

        x2 = x.reshape(T, D)
        h, p1, n1, p2, r2 = _peer_select(x2, norm_ffn[layer].astype(F32)[None], peer_wq[layer].astype(BF16),
                                         peer_keys[layer].astype(BF16), tt_sel)
        x = _peer_dense(x2, h, p1, n1, p2, r2, peer_u[layer].astype(BF16), peer_v[layer].astype(BF16).T,
                        tt_peer, te).reshape(B, S, D)
    return x
```

```python
import functools
import math

import numpy as np
import jax
import jax.numpy as jnp
from jax import lax
from jax.experimental import pallas as pl
from jax.experimental.pallas import tpu as pltpu

F32 = jnp.float32
BF16 = jnp.bfloat16

LANES = 128
GROUP = 256
HEAD_DIM = 64
N_HEADS = GROUP // HEAD_DIM
DIFF_HALF = HEAD_DIM // 2
S5_CH = 16
S5_GROUPS = GROUP // S5_CH
S5_STATE = 64
S5_WIDE = S5_GROUPS * S5_STATE
DIL_BLOCK = 128
DIL_PAIRS = ((128, 1), (512, 4), (2048, 16))
DIL_SUPER = DIL_BLOCK * DIL_PAIRS[-1][1]
REL_BUCKETS = 32
REL_MAX_EXACT = REL_BUCKETS // 2
REL_MAX_DIST = 2048
N_KEYS = 128
PEER_HEADS = 8
PEER_TOPK = 16
RMS_EPS = 1e-6
NEG = -1e30
LOG2E = math.log2(math.e)
GELU_A = -2.0 * math.sqrt(2.0 / math.pi) * LOG2E
GELU_B = GELU_A * 0.044715
PEER_SUB = 256
VMEM_LIMIT = 56 * 1024 * 1024

IN_COLS = 10 * GROUP + LANES
FORGET_COL = 10 * GROUP


def _cparams(sem):
    return pltpu.CompilerParams(dimension_semantics=sem, vmem_limit_bytes=VMEM_LIMIT)


def _dot(a, b):
    return jnp.dot(a, b, preferred_element_type=F32)


def _dot_nt(a, b):
    return lax.dot_general(a, b, (((1,), (1,)), ((), ())), preferred_element_type=F32)


def _split3(x):
    hi = x.astype(BF16)
    r = x - hi.astype(F32)
    mid = r.astype(BF16)
    lo = (r - mid.astype(F32)).astype(BF16)
    return hi, mid, lo


def _gelu(x):
    return 0.5 * x * (1.0 + jnp.tanh(math.sqrt(2.0 / math.pi) * (x + 0.044715 * (x * x * x))))


def _t5_bucket(dist):
    n = jnp.maximum(dist, 0)
    nf = jnp.maximum(n, REL_MAX_EXACT).astype(F32)
    large = REL_MAX_EXACT + (jnp.log(nf / REL_MAX_EXACT) / math.log(REL_MAX_DIST / REL_MAX_EXACT)
                             * (REL_BUCKETS - REL_MAX_EXACT)).astype(jnp.int32)
    large = jnp.minimum(large, REL_BUCKETS - 1)
    return jnp.where(n < REL_MAX_EXACT, n, large)


def _bucket_saturation():
    n = REL_MAX_EXACT * (REL_MAX_DIST / REL_MAX_EXACT) ** ((REL_BUCKETS - 1 - REL_MAX_EXACT)
                                                            / (REL_BUCKETS - REL_MAX_EXACT))
    return int(math.ceil(n)) + 2


def _inproj_kernel(x_ref, g_ref, w_ref, bf_ref, gains_ref, ones64_ref, ones32_ref, tri_ref, aug_ref,
                   fq_ref, fk_ref, fv_ref, cq1_ref, cq2_ref, ck_ref, cv_ref,
                   dq_ref, dk_ref, dv_ref, su_ref, carry_ref):
    @pl.when(pl.program_id(1) == 0)
    def _():
        carry_ref[...] = jnp.zeros_like(carry_ref)

    x = x_ref[0]
    tm = x.shape[0]
    h = x * lax.rsqrt(jnp.mean(x * x, axis=-1, keepdims=True) + RMS_EPS) * g_ref[...]
    z = _dot(h.astype(BF16), w_ref[...])

    def group(idx):
        return z[:, idx * GROUP:(idx + 1) * GROUP]

    def head_norm(zg, ones_ref, width, gain_row):
        ss = _dot((zg * zg).astype(BF16), ones_ref[...])
        return zg * lax.rsqrt(ss * (1.0 / width) + RMS_EPS) * gains_ref[gain_row:gain_row + 1, :]

    lane = lax.broadcasted_iota(jnp.int32, (tm, LANES), 1)

    def head_chunk(zg, hh):
        c = zg[:, (hh // 2) * LANES:(hh // 2 + 1) * LANES]
        if hh % 2:
            c = pltpu.roll(c, HEAD_DIM, axis=1)
        return c

    f = z[:, FORGET_COL:FORGET_COL + LANES] + bf_ref[...]
    ls = jnp.minimum(f, 0.0) - jnp.log1p(jnp.exp(-jnp.abs(f)))
    p0, p1, p2 = _split3(ls)
    tri = tri_ref[...]
    cum = (_dot(tri, p0) + _dot(tri, p1)) + _dot(tri, p2) + carry_ref[...]
    carry_ref[...] = cum[tm - 1:tm, :]

    fq = head_norm(group(0), ones64_ref, HEAD_DIM, 0)
    fk = head_norm(group(1), ones64_ref, HEAD_DIM, 1)
    fv = group(2)
    dq = head_norm(group(3), ones64_ref, HEAD_DIM, 2)
    dk = head_norm(group(4), ones64_ref, HEAD_DIM, 3)
    cq = head_norm(group(6), ones32_ref, DIFF_HALF, 4)
    ck = head_norm(group(7), ones32_ref, DIFF_HALF, 5)
    cv = group(8)

    for hf in range(2):
        dq_ref[0, hf] = dq[:, hf * LANES:(hf + 1) * LANES]
        dk_ref[0, hf] = dk[:, hf * LANES:(hf + 1) * LANES]
        dv_ref[0, hf] = group(5)[:, hf * LANES:(hf + 1) * LANES]
    su_ref[...] = group(9)

    one = jnp.ones((tm, LANES), F32)
    zero = jnp.zeros((tm, LANES), F32)
    for hh in range(N_HEADS):
        c = jnp.broadcast_to(cum[:, hh:hh + 1], (tm, LANES)) * LOG2E
        c0, c1, c2 = _split3(c)
        c0, c1, c2 = c0.astype(F32), c1.astype(F32), c2.astype(F32)
        qa = jnp.where(lane == 64, c0, jnp.where(lane == 65, c1, jnp.where(lane == 66, c2,
             jnp.where(lane < 70, one, zero))))
        ka = jnp.where(lane == 67, -c0, jnp.where(lane == 68, -c1, jnp.where(lane == 69, -c2,
             jnp.where(lane < 70, one, zero))))
        fq_ref[0, hh] = jnp.where(lane < HEAD_DIM, head_chunk(fq, hh), qa).astype(BF16)
        fk_ref[0, hh] = jnp.where(lane < HEAD_DIM, head_chunk(fk, hh), ka).astype(BF16)
        vaug = aug_ref[0:1, :]
        fv_ref[0, hh] = jnp.where(lane < HEAD_DIM, head_chunk(fv, hh), vaug).astype(BF16)
        qaug = aug_ref[1:2, :]
        cqc = head_chunk(cq, hh)
        cq1_ref[0, hh] = jnp.where(lane < DIFF_HALF, cqc, qaug).astype(BF16)
        cq2_ref[0, hh] = jnp.where((lane >= DIFF_HALF) & (lane < HEAD_DIM), cqc, qaug).astype(BF16)
        kaug = aug_ref[2 + hh:3 + hh, :]
        ck_ref[0, hh] = jnp.where(lane < HEAD_DIM, head_chunk(ck, hh), kaug).astype(BF16)
        cv_ref[0, hh] = jnp.where(lane < HEAD_DIM, head_chunk(cv, hh), vaug).astype(BF16)


def _inproj(x, g, w, bf, gains, ones64, ones32, tri, aug, tm):
    B, S, D = x.shape
    nt = S // tm
    hm = jax.ShapeDtypeStruct((B, N_HEADS, S, LANES), BF16)
    nat = jax.ShapeDtypeStruct((B, 2, S, LANES), F32)
    hm_spec = pl.BlockSpec((1, N_HEADS, tm, LANES), lambda b, i: (b, 0, i, 0))
    nat_spec = pl.BlockSpec((1, 2, tm, LANES), lambda b, i: (b, 0, i, 0))

    def full(a):
        return pl.BlockSpec(a.shape, lambda b, i: (0,) * a.ndim)

    return pl.pallas_call(
        _inproj_kernel,
        grid=(B, nt),
        in_specs=[pl.BlockSpec((1, tm, D), lambda b, i: (b, i, 0)),
                  full(g), full(w), full(bf), full(gains), full(ones64), full(ones32), full(tri), full(aug)],
        out_specs=[hm_spec] * 7 + [nat_spec] * 3 + [pl.BlockSpec((tm, GROUP), lambda b, i: (i, b))],
        out_shape=[hm] * 7 + [nat] * 3 + [jax.ShapeDtypeStruct((S, B * GROUP), F32)],
        scratch_shapes=[pltpu.VMEM((1, LANES), F32)],
        compiler_params=_cparams(("arbitrary", "arbitrary")),
        name="inproj",
    )(x, g, w, bf, gains, ones64, ones32, tri, aug)


def _flash_step(q, k, v, m, acc, bias=None):
    s = _dot_nt(q, k)
    if bias is not None:
        s = s + bias
    m_new = jnp.maximum(m, jnp.max(s, axis=1, keepdims=True))
    alpha = jnp.exp2(m - m_new)
    p = jnp.exp2(s - m_new)
    acc = alpha * acc + _dot(p.astype(BF16), v)
    return m_new, acc


def _fox_kernel(q_ref, k_ref, v_ref, o_ref, *, tq, tk, tkd):
    i = pl.program_id(2)
    q = q_ref[0, 0]

    def kv(off, size):
        return k_ref[0, 0, pl.ds(off, size), :], v_ref[0, 0, pl.ds(off, size), :]

    def body(j, carry):
        m, acc = carry
        k, v = kv(pl.multiple_of(j * tk, tk), tk)
        return _flash_step(q, k, v, m, acc)

    m0 = jnp.full((tq, 1), NEG, F32)
    acc0 = jnp.zeros((tq, LANES), F32)
    m, acc = lax.fori_loop(0, i * (tq // tk), body, (m0, acc0))
    for d in range(tq // tkd):
        rows = tq - d * tkd
        k, v = kv(pl.multiple_of(i * tq + d * tkd, tkd), tkd)
        row = lax.broadcasted_iota(jnp.int32, (rows, tkd), 0)
        col = lax.broadcasted_iota(jnp.int32, (rows, tkd), 1)
        mask = jnp.where(col <= row, 0.0, NEG)
        m_d, acc_d = _flash_step(q[d * tkd:], k, v, m[d * tkd:], acc[d * tkd:], mask)
        if d:
            m = jnp.concatenate([m[:d * tkd], m_d], axis=0)
            acc = jnp.concatenate([acc[:d * tkd], acc_d], axis=0)
        else:
            m, acc = m_d, acc_d
    den = jnp.broadcast_to(acc[:, HEAD_DIM:HEAD_DIM + 1], (tq, LANES))
    o_ref[0, 0] = (acc / den).astype(BF16)


def _fox_attention(q, k, v, tq, tk, tkd):
    B, H, S, _ = q.shape
    return pl.pallas_call(
        functools.partial(_fox_kernel, tq=tq, tk=tk, tkd=tkd),
        grid=(B, H, S // tq),
        in_specs=[pl.BlockSpec((1, 1, tq, LANES), lambda b, h, i: (b, h, i, 0)),
                  pl.BlockSpec((1, 1, S, LANES), lambda b, h, i: (b, h, 0, 0)),
                  pl.BlockSpec((1, 1, S, LANES), lambda b, h, i: (b, h, 0, 0))],
        out_specs=pl.BlockSpec((1, 1, tq, LANES), lambda b, h, i: (b, h, i, 0)),
        out_shape=jax.ShapeDtypeStruct((B, H, S, LANES), BF16),
        compiler_params=_cparams(("arbitrary", "arbitrary", "arbitrary")),
        name="fox_attention",
    )(q, k, v)


def _diff_kernel(lam_ref, q1_ref, q2_ref, k_ref, v_ref, bias_ref, gain_ref, o_ref, *, t, n_near):
    i = pl.program_id(2)
    q = jnp.concatenate([q1_ref[0, 0], q2_ref[0, 0]], axis=0)
    n_pair = jnp.maximum(i - (n_near - 1), 0) // 2

    def kv(j, size):
        off = pl.multiple_of(j * size, size)
        return k_ref[0, 0, pl.ds(off, size), :], v_ref[0, 0, pl.ds(off, size), :]

    def far(j, carry):
        m, acc = carry
        k, v = kv(j, 2 * t)
        return _flash_step(q, k, v, m, acc)

    def near(j, carry):
        m, acc = carry
        k, v = kv(j, t)
        b = bias_ref[0, jnp.minimum(i - j, n_near)]
        return _flash_step(q, k, v, m, acc, jnp.concatenate([b, b], axis=0))

    m0 = jnp.full((2 * t, 1), NEG, F32)
    a0 = jnp.zeros((2 * t, LANES), F32)
    carry = lax.fori_loop(0, n_pair, far, (m0, a0))
    _, acc = lax.fori_loop(2 * n_pair, i + 1, near, carry)
    a1, a2 = acc[:t], acc[t:]
    d1 = jnp.broadcast_to(a1[:, HEAD_DIM:HEAD_DIM + 1], (t, LANES))
    d2 = jnp.broadcast_to(a2[:, HEAD_DIM:HEAD_DIM + 1], (t, LANES))
    lane = lax.broadcasted_iota(jnp.int32, (t, LANES), 1)
    o = jnp.where(lane < HEAD_DIM, a1 / d1 - lam_ref[0] * (a2 / d2), 0.0)
    ms = jnp.sum(o * o, axis=1, keepdims=True) * (1.0 / HEAD_DIM)
    o_ref[0, 0] = (o * lax.rsqrt(ms + RMS_EPS) * gain_ref[...] * lam_ref[1]).astype(BF16)


def _diff_attention(lam, q1, q2, k, v, bias, gain, t):
    B, H, S, _ = q1.shape
    n_near = bias.shape[1] - 1
    qspec = pl.BlockSpec((1, 1, t, LANES), lambda b, h, i: (b, h, i, 0))
    kspec = pl.BlockSpec((1, 1, S, LANES), lambda b, h, i: (b, h, 0, 0))
    return pl.pallas_call(
        functools.partial(_diff_kernel, t=t, n_near=n_near),
        grid=(B, H, S // t),
        in_specs=[pl.BlockSpec(memory_space=pltpu.SMEM), qspec, qspec, kspec, kspec,
                  pl.BlockSpec((1, n_near + 1, t, t), lambda b, h, i: (h, 0, 0, 0)),
                  pl.BlockSpec((1, LANES), lambda b, h, i: (0, 0))],
        out_specs=qspec,
        out_shape=jax.ShapeDtypeStruct((B, H, S, LANES), BF16),
        compiler_params=_cparams(("arbitrary", "arbitrary", "arbitrary")),
        name="diff_attention",
    )(lam, q1, q2, k, v, bias, gain)


def _dil_kernel(qc_ref, kp_ref, kc_ref, vp_ref, vc_ref, b0_ref, b1_ref, b2_ref, o_ref, num_ref, st_ref):
    first = pl.program_id(1) == 0
    w = DIL_BLOCK
    lane = lax.broadcasted_iota(jnp.int32, (w, GROUP), 1)
    lane_s = lax.broadcasted_iota(jnp.int32, (w, LANES), 1)
    heads = [(lane >= hh * HEAD_DIM) & (lane < (hh + 1) * HEAD_DIM) for hh in range(N_HEADS)]
    ones = jnp.ones((w, LANES), BF16)

    def rows(cur_ref, prev_ref, off, res, dil):
        ref, base = (cur_ref, off) if off >= 0 else (prev_ref, off + DIL_SUPER)
        if dil == 1:
            parts = [ref[0, hf, base:base + w, :] for hf in range(2)]
        else:
            parts = [ref[0, hf, pl.ds(base + res, w, stride=dil), :] for hf in range(2)]
        return jnp.concatenate(parts, axis=1).astype(BF16)

    for n, (bias_ref, (_, dil)) in enumerate(zip((b0_ref, b1_ref, b2_ref), DIL_PAIRS)):
        span = w * dil
        for u in range(DIL_SUPER // span):
            off = u * span
            for res in range(dil):
                q = rows(qc_ref, qc_ref, off, res, dil)
                kc, vc = rows(kc_ref, kp_ref, off, res, dil), rows(vc_ref, vp_ref, off, res, dil)
                kp, vp = rows(kc_ref, kp_ref, off - span, res, dil), rows(vc_ref, vp_ref, off - span, res, dil)
                qs = jnp.concatenate([jnp.where(heads[hh], q, jnp.zeros_like(q)) for hh in range(N_HEADS)], axis=0)
                sp = _dot_nt(qs, kp) + bias_ref[:, 0:w]
                if u == 0:
                    sp = sp + jnp.where(first, NEG, 0.0)
                sc = _dot_nt(qs, kc) + bias_ref[:, w:2 * w]
                m = jnp.maximum(jnp.max(sp, axis=1, keepdims=True), jnp.max(sc, axis=1, keepdims=True))
                pp = jnp.exp(sp - m).astype(BF16)
                pc = jnp.exp(sc - m).astype(BF16)
                den = _dot(pp, ones) + _dot(pc, ones)
                o = _dot(pp, vp) + _dot(pc, vc)
                num = jnp.zeros((w, GROUP), F32)
                st = jnp.zeros((w, LANES), F32)
                for hh in range(N_HEADS):
                    rs = slice(hh * w, (hh + 1) * w)
                    num = jnp.where(heads[hh], o[rs], num)
                    st = jnp.where(lane_s == hh, den[rs], jnp.where(lane_s == N_HEADS + hh, m[rs], st))
                dst = slice(off, off + w) if dil == 1 else pl.ds(off + res, w, stride=dil)
                for hf in range(2):
                    num_ref[n, hf, dst, :] = num[:, hf * LANES:(hf + 1) * LANES]
                st_ref[n, dst, :] = st

    tm = 2 * w
    lane2 = lax.broadcasted_iota(jnp.int32, (tm, GROUP), 1)
    for c in range(DIL_SUPER // tm):
        sl = slice(c * tm, (c + 1) * tm)
        nums = [jnp.concatenate([num_ref[n, 0, sl, :], num_ref[n, 1, sl, :]], axis=1) for n in range(3)]
        sts = [st_ref[n, sl, :] for n in range(3)]
        out = jnp.zeros((tm, GROUP), F32)
        for hh in range(N_HEADS):
            ms = [s[:, N_HEADS + hh:N_HEADS + hh + 1] for s in sts]
            ds = [s[:, hh:hh + 1] for s in sts]
            mx = jnp.maximum(jnp.maximum(ms[0], ms[1]), ms[2])
            ws = [jnp.exp(m - mx) for m in ms]
            den = ws[0] * ds[0] + ws[1] * ds[1] + ws[2] * ds[2]
            o = (ws[0] * nums[0] + ws[1] * nums[1] + ws[2] * nums[2]) / den
            out = jnp.where((lane2 >= hh * HEAD_DIM) & (lane2 < (hh + 1) * HEAD_DIM), o, out)
        o_ref[0, sl, :] = out.astype(BF16)


def _dilated_attention(q, k, v, biases):
    B, _, S, _ = q.shape
    cur = pl.BlockSpec((1, 2, DIL_SUPER, LANES), lambda b, n: (b, 0, n, 0))
    prev = pl.BlockSpec((1, 2, DIL_SUPER, LANES), lambda b, n: (b, 0, jnp.maximum(n - 1, 0), 0))
    bspec = pl.BlockSpec(biases[0].shape, lambda b, n: (0, 0))
    return pl.pallas_call(
        _dil_kernel,
        grid=(B, S // DIL_SUPER),
        in_specs=[cur, prev, cur, prev, cur, bspec, bspec, bspec],
        out_specs=pl.BlockSpec((1, DIL_SUPER, GROUP), lambda b, n: (b, n, 0)),
        out_shape=jax.ShapeDtypeStruct((B, S, GROUP), BF16),
        scratch_shapes=[pltpu.VMEM((3, 2, DIL_SUPER, LANES), F32), pltpu.VMEM((3, DIL_SUPER, LANES), F32)],
        compiler_params=_cparams(("arbitrary", "arbitrary")),
        name="dilated_attention",
    )(q, k, k, v, v, *biases)


def _s5_kernel(u_ref, are_ref, aim_ref, bre_ref, bim_ref, cre_ref, cim_ref, d_ref, gw_ref, gb_ref,
               o_ref, xr_ref, xi_ref, bur_ref, bui_ref, sr_ref, si_ref, *, nb, tc):
    @pl.when(pl.program_id(0) == 0)
    def _():
        sr_ref[...] = jnp.zeros_like(sr_ref)
        si_ref[...] = jnp.zeros_like(si_ref)

    u = u_ref[...]
    ub = u.astype(BF16)
    bur_ref[...] = _dot(ub, bre_ref[...])
    bui_ref[...] = _dot(ub, bim_ref[...])
    ar = jnp.broadcast_to(are_ref[...], (nb, S5_WIDE))
    ai = jnp.broadcast_to(aim_ref[...], (nb, S5_WIDE))

    def step(t, carry):
        xr, xi = carry
        rows = pl.ds(pl.multiple_of(t * nb, nb), nb)
        nr = ar * xr - ai * xi + bur_ref[rows, :]
        ni = ar * xi + ai * xr + bui_ref[rows, :]
        xr_ref[rows, :] = nr
        xi_ref[rows, :] = ni
        return nr, ni

    xr, xi = lax.fori_loop(0, tc, step, (sr_ref[...], si_ref[...]))
    sr_ref[...] = xr
    si_ref[...] = xi
    y = (_dot(xr_ref[...].astype(BF16), cre_ref[...]) - _dot(xi_ref[...].astype(BF16), cim_ref[...])
         + d_ref[...] * u)
    g = _gelu(y)
    gate = _dot(g.astype(BF16), gw_ref[...]) + gb_ref[...]
    o_ref[...] = (g * (1.0 / (1.0 + jnp.exp(-gate)))).astype(BF16)


def _s5(u2, are, aim, bre, bim, cre, cim, d, gw, gb, nb, tc):
    rows = u2.shape[0]
    tr = tc * nb

    def full(a):
        return pl.BlockSpec(a.shape, lambda i: (0,) * a.ndim)

    consts = (are, aim, bre, bim, cre, cim, d, gw, gb)
    return pl.pallas_call(
        functools.partial(_s5_kernel, nb=nb, tc=tc),
        grid=(rows // tr,),
        in_specs=[pl.BlockSpec((tr, GROUP), lambda i: (i, 0))] + [full(a) for a in consts],
        out_specs=pl.BlockSpec((tr, GROUP), lambda i: (i, 0)),
        out_shape=jax.ShapeDtypeStruct((rows, GROUP), BF16),
        scratch_shapes=[pltpu.VMEM((tr, S5_WIDE), F32)] * 4 + [pltpu.VMEM((nb, S5_WIDE), F32)] * 2,
        compiler_params=_cparams(("arbitrary",)),
        name="s5_mixer",
    )(u2, *consts)


def _outproj_kernel(x_ref, fox_ref, dil_ref, dif_ref, s5_ref, wf_ref, wd_ref, wc_ref, ws_ref, o_ref):
    acc = x_ref[0] + _dot(dil_ref[0], wd_ref[...]) + _dot(s5_ref[...], ws_ref[...])
    for hh in range(N_HEADS):
        acc = acc + _dot(fox_ref[0, hh], wf_ref[hh]) + _dot(dif_ref[0, hh], wc_ref[hh])
    o_ref[0] = acc


def _outproj(x, fox, dil, dif, s5, wf, wd, wc, ws, tm):
    B, S, D = x.shape
    xspec = pl.BlockSpec((1, tm, D), lambda b, i: (b, i, 0))
    hspec = pl.BlockSpec((1, N_HEADS, tm, LANES), lambda b, i: (b, 0, i, 0))

    def full(a):
        return pl.BlockSpec(a.shape, lambda b, i: (0,) * a.ndim)

    return pl.pallas_call(
        _outproj_kernel,
        grid=(B, S // tm),
        in_specs=[xspec, hspec, pl.BlockSpec((1, tm, GROUP), lambda b, i: (b, i, 0)), hspec,
                  pl.BlockSpec((tm, GROUP), lambda b, i: (i, b)),
                  full(wf), full(wd), full(wc), full(ws)],
        out_specs=xspec,
        out_shape=jax.ShapeDtypeStruct((B, S, D), F32),
        compiler_params=_cparams(("arbitrary", "arbitrary")),
        name="outproj",
    )(x, fox, dil, dif, s5, wf, wd, wc, ws)


def _staircase(n):
    return [(a, b) for a in range(n) for b in range(n) if (a + 1) * (b + 1) <= n]


def _peer_select_kernel(x_ref, g_ref, wq_ref, keys_ref, h_ref, p1_ref, n1_ref, p2_ref, r2_ref):
    x = x_ref[...]
    tt = x.shape[0]
    h = (x * lax.rsqrt(jnp.mean(x * x, axis=-1, keepdims=True) + RMS_EPS) * g_ref[...]).astype(BF16)
    h_ref[...] = h
    q = _dot(h, wq_ref[...]).astype(BF16)
    n_top = PEER_TOPK + 1
    pairs = _staircase(n_top)
    chains = [(hh, half) for hh in range(PEER_HEADS) for half in range(2)]
    scores = {}
    for hh, half in chains:
        c = (hh * 2 + half) * N_KEYS
        scores[hh, half] = _dot_nt(keys_ref[hh, half], q[:, c:c + N_KEYS])
    work = dict(scores)
    tops = {ch: [] for ch in chains}
    for kk in range(n_top):
        for hh, half in chains:
            s = work[hh, half]
            mx = jnp.max(s, axis=0, keepdims=True)
            tops[hh, half].append(mx)
            work[hh, half] = jnp.where(s == mx, NEG * (1.0 + kk / 64.0), s)
    cands = [jnp.concatenate([tops[hh, 0][a] + tops[hh, 1][b] for a, b in pairs]
                             + [jnp.full((8 - len(pairs) % 8, tt), NEG, F32)], axis=0) for hh in range(PEER_HEADS)]
    bests = [[] for _ in range(PEER_HEADS)]
    for _ in range(n_top):
        for hh in range(PEER_HEADS):
            mx = jnp.max(cands[hh], axis=0, keepdims=True)
            bests[hh].append(mx)
            cands[hh] = jnp.where(cands[hh] == mx, NEG, cands[hh])
    for hh in range(PEER_HEADS):
        best = bests[hh]
        tops_h = [tops[hh, 0], tops[hh, 1]]
        scores_h = [scores[hh, 0], scores[hh, 1]]
        left = work[hh, 1]
        rank = jnp.where(left < 0.5 * NEG, (left * (1.0 / NEG) - 1.0) * 64.0, float(N_KEYS))
        z = jnp.zeros((1, tt), F32)
        for kk in range(PEER_TOPK):
            z = z + jnp.exp(best[kk] - best[0])
        cut = 0.5 * (best[PEER_TOPK - 1] + best[PEER_TOPK])
        n1 = jnp.zeros((N_KEYS, tt), F32)
        for b in range(PEER_TOPK):
            n1 = jnp.where(scores_h[0] >= cut - tops_h[1][b], b + 1.0, n1)
        p1_ref[hh] = jnp.exp(scores_h[0] - tops_h[0][0]) * (1.0 / z)
        n1_ref[hh] = n1
        p2_ref[hh] = pltpu.bitcast(jnp.exp(scores_h[1] - tops_h[1][0]).astype(BF16), jnp.uint32)
        r2_ref[hh] = pltpu.bitcast(rank.astype(BF16), jnp.uint32)


def _peer_select(x2, g, wq, keys, tt):
    T, D = x2.shape
    rows = pl.BlockSpec((PEER_HEADS, N_KEYS, tt), lambda i: (0, 0, i))
    words = pl.BlockSpec((PEER_HEADS, N_KEYS // 2, tt), lambda i: (0, 0, i))
    return pl.pallas_call(
        _peer_select_kernel,
        grid=(T // tt,),
        in_specs=[pl.BlockSpec((tt, D), lambda i: (i, 0)),
                  pl.BlockSpec(g.shape, lambda i: (0, 0)),
                  pl.BlockSpec(wq.shape, lambda i: (0, 0)),
                  pl.BlockSpec(keys.shape, lambda i: (0, 0, 0, 0))],
        out_specs=[pl.BlockSpec((tt, D), lambda i: (i, 0)), rows, rows, words, words],
        out_shape=[jax.ShapeDtypeStruct((T, D), BF16),
                   jax.ShapeDtypeStruct((PEER_HEADS, N_KEYS, T), F32),
                   jax.ShapeDtypeStruct((PEER_HEADS, N_KEYS, T), F32),
                   jax.ShapeDtypeStruct((PEER_HEADS, N_KEYS // 2, T), jnp.uint32),
                   jax.ShapeDtypeStruct((PEER_HEADS, N_KEYS // 2, T), jnp.uint32)],
        compiler_params=_cparams(("arbitrary",)),
        name="peer_select",
    )(x2, g, wq, keys)


def _gelu_gate(x, g):
    ga, gb, one = (jnp.asarray(c, x.dtype) for c in (GELU_A, GELU_B, 1.0))
    e = jnp.exp2(x * (ga + gb * (x * x)))
    return x * g / (one + e)


def _peer_dense_kernel(x_ref, h_ref, p1_ref, n1_ref, p2_ref, r2_ref, u_ref, vt_ref, o_ref,
                       a_ref, w_ref, acc_ref, *, te, tt, n_e):
    s = pl.program_id(0)

    @pl.when(s == 0)
    def _():
        a_ref[...] = jnp.zeros_like(a_ref)
        w_ref[...] = jnp.zeros_like(w_ref)
        acc_ref[...] = jnp.zeros_like(acc_ref)

    slot_a = s % 2
    slot_b = 1 - slot_a
    j_b = jnp.maximum(s - 1, 0) % n_e
    j_c = jnp.maximum(s - 2, 0) % n_e

    h = h_ref[...]
    for q in range(te // PEER_SUB):
        a_ref[slot_a, q * PEER_SUB:(q + 1) * PEER_SUB, :] = _dot_nt(u_ref[q * PEER_SUB:(q + 1) * PEER_SUB, :], h)

    acc_ref[...] = jnp.where(j_c == 0, 0.0, acc_ref[...]) + _dot(vt_ref[...], w_ref[slot_a])

    rows_per_step = te // N_KEYS
    rows = pl.ds(pl.multiple_of(j_b * rows_per_step, rows_per_step), rows_per_step)
    for r in range(rows_per_step):
        for c in range(tt // LANES):
            cols = slice(c * LANES, (c + 1) * LANES)
            g = jnp.zeros((N_KEYS, LANES), BF16)
            for hh in range(PEER_HEADS):
                gate = jnp.broadcast_to(p1_ref[hh, rows, cols][r:r + 1], (N_KEYS, LANES)).astype(BF16)
                count = jnp.broadcast_to(n1_ref[hh, rows, cols][r:r + 1], (N_KEYS, LANES)).astype(BF16)
                r2 = pltpu.bitcast(r2_ref[hh, :, cols], BF16)
                p2 = pltpu.bitcast(p2_ref[hh, :, cols], BF16)
                sel = jnp.where(r2 < count, p2, jnp.zeros((), BF16))
                g = g + sel * gate
            a = a_ref[slot_b, r * N_KEYS:(r + 1) * N_KEYS, cols].astype(BF16)
            w_ref[slot_b, r * N_KEYS:(r + 1) * N_KEYS, cols] = _gelu_gate(a, g)

    @pl.when((j_c == n_e - 1) & (s >= 2))
    def _():
        o_ref[...] = x_ref[...] + acc_ref[...].T


def _peer_dense(x2, h, p1, n1, p2, r2, u, vt, tt, te):
    T, D = x2.shape
    n_e = u.shape[0] // te
    last = (T // tt) * n_e - 1

    def tok(lag):
        return lambda s: jnp.clip(s - lag, 0, last) // n_e

    def tile(lag):
        return lambda s: jnp.clip(s - lag, 0, last) % n_e

    rows = pl.BlockSpec((PEER_HEADS, N_KEYS, tt), lambda s: (0, 0, tok(1)(s)))
    words = pl.BlockSpec((PEER_HEADS, N_KEYS // 2, tt), lambda s: (0, 0, tok(1)(s)))
    return pl.pallas_call(
        functools.partial(_peer_dense_kernel, te=te, tt=tt, n_e=n_e),
        grid=(last + 3,),
        in_specs=[pl.BlockSpec((tt, D), lambda s: (tok(2)(s), 0)),
                  pl.BlockSpec((tt, D), lambda s: (tok(0)(s), 0)),
                  rows, rows, words, words,
                  pl.BlockSpec((te, D), lambda s: (tile(0)(s), 0)),
                  pl.BlockSpec((D, te), lambda s: (0, tile(2)(s)))],
        out_specs=pl.BlockSpec((tt, D), lambda s: (tok(2)(s), 0)),
        out_shape=jax.ShapeDtypeStruct((T, D), F32),
        scratch_shapes=[pltpu.VMEM((2, te, tt), F32), pltpu.VMEM((2, te, tt), BF16), pltpu.VMEM((D, tt), F32)],
        compiler_params=_cparams(("arbitrary",)),
        name="peer_dense",
    )(x2, h, p1, n1, p2, r2, u, vt)


def _block_ones(width):
    idx = np.arange(GROUP) // width
    return jnp.asarray(idx[:, None] == idx[None, :], BF16)


def _pad_head_rows(w):
    w = w.reshape(N_HEADS, HEAD_DIM, -1)
    return jnp.concatenate([w, jnp.zeros_like(w)], axis=1).astype(BF16)


def _dil_bias(rel_bias, window, dil):
    ws = window // dil
    j = jnp.arange(DIL_BLOCK)[:, None]
    m = jnp.arange(2 * DIL_BLOCK)[None, :]
    sub = DIL_BLOCK + j - m
    band = (sub >= 0) & (sub <= ws)
    bias = rel_bias[_t5_bucket(dil * jnp.clip(sub, 0, ws))].transpose(2, 0, 1).astype(F32)
    return jnp.where(band[None], bias, NEG).reshape(N_HEADS * DIL_BLOCK, 2 * DIL_BLOCK)


def _diff_bias(rel_bias, t, n_near):
    dist = jnp.arange(-(t - 1), n_near * t + 2)
    b = rel_bias[_t5_bucket(dist)].astype(F32) - rel_bias[REL_BUCKETS - 1].astype(F32)
    b = jnp.where((dist >= 0)[:, None], b, NEG).T
    rev = b[:, ::-1]
    n = b.shape[1]
    tiles = []
    for delta in range(n_near):
        start = n - 1 - (delta * t + 2 * t - 2)
        u = jnp.pad(rev[:, start:start + 2 * t - 1], ((0, 0), (0, 1)))
        flat = jnp.tile(u, (1, t))[:, t - 1:t - 1 + t * (2 * t - 1)]
        tiles.append(flat.reshape(-1, t, 2 * t - 1)[:, :, :t])
    tiles.append(jnp.zeros_like(tiles[0]))
    return jnp.stack(tiles, axis=1)


def _s5_params(a_re, a_im, log_dt, b_re, b_im, c_re, c_im, d_skip):
    G, N, C = S5_GROUPS, S5_STATE, S5_CH
    dt = jnp.exp(log_dt.astype(F32))[:, None]
    ar, ai = a_re.astype(F32), a_im.astype(F32)
    mag = jnp.exp(dt * ar)
    abar_re, abar_im = mag * jnp.cos(dt * ai), mag * jnp.sin(dt * ai)
    den = ar * ar + ai * ai
    nr, ni = abar_re - 1.0, abar_im
    coef_re = (nr * ar + ni * ai) / den
    coef_im = (ni * ar - nr * ai) / den
    br, bi = b_re.astype(F32), b_im.astype(F32)
    bbar_re = coef_re[..., None] * br - coef_im[..., None] * bi
    bbar_im = coef_re[..., None] * bi + coef_im[..., None] * br
    eye = jnp.eye(G, dtype=F32)

    def b_blockdiag(bb):
        return jnp.einsum('gnc,gk->gckn', bb, eye).reshape(G * C, G * N).astype(BF16)

    def c_blockdiag(cc):
        return jnp.einsum('gcn,gk->gnkc', cc.astype(F32), eye).reshape(G * N, G * C).astype(BF16)

    return (abar_re.reshape(1, G * N), abar_im.reshape(1, G * N), b_blockdiag(bbar_re), b_blockdiag(bbar_im),
            c_blockdiag(c_re), c_blockdiag(c_im), d_skip.astype(F32).reshape(1, G * C))


def _pick(n, pref):
    for c in pref:
        if n % c == 0:
            return c
    raise ValueError("unsupported size %d" % n)


def kernel(x, w_in, b_forget, w_out, norm_mix, norm_ffn, fox_q_norm, fox_k_norm, dil_q_norm, dil_k_norm, diff_q_norm, diff_k_norm, diff_lambda, diff_out_norm, rel_bias, s5_a_re, s5_a_im, s5_log_dt, s5_b_re, s5_b_im, s5_c_re, s5_c_im, s5_d, s5_glu_w, s5_glu_b, peer_wq, peer_keys, peer_u, peer_v):
    B, S, D = x.shape
    depth = w_in.shape[0]
    T = B * S
    assert S % (DIL_BLOCK * DIL_PAIRS[-1][1]) == 0 and D % LANES == 0
    tm = _pick(S, (512, 256, 128))
    t_diff = _pick(S, (512, 256))
    tq_fox = _pick(S, (1024, 512, 256))
    tkd_fox = min(tq_fox, 512)
    tc = _pick(S, (128, 64))
    tt_sel = _pick(T, (512, 256))
    tt_peer = _pick(T, (512, 256))
    te = 1024

    n_sat = _bucket_saturation()
    n_near = min(-(-(n_sat + t_diff - 1) // t_diff), S // t_diff)
    rel_bias = rel_bias.astype(F32)
    dil_bias = [_dil_bias(rel_bias[:, :N_HEADS], w, r) for w, r in DIL_PAIRS]
    diff_bias = _diff_bias(rel_bias[:, N_HEADS:], t_diff, n_near) * LOG2E
    far = rel_bias[REL_BUCKETS - 1, N_HEADS:] * LOG2E
    far_hi = far.astype(BF16).astype(F32)
    lane = jnp.arange(LANES)
    aug = jnp.zeros((2 + N_HEADS, LANES), F32)
    aug = aug.at[0].set(jnp.where(lane == HEAD_DIM, 1.0, 0.0))
    aug = aug.at[1].set(jnp.where((lane == HEAD_DIM) | (lane == HEAD_DIM + 1), 1.0, 0.0))
    aug = aug.at[2:].set(jnp.where(lane[None] == HEAD_DIM, far_hi[:, None],
                                   jnp.where(lane[None] == HEAD_DIM + 1, (far - far_hi)[:, None], 0.0)))
    ones64, ones32 = _block_ones(HEAD_DIM), _block_ones(DIFF_HALF)
    tri = jnp.asarray(np.tril(np.ones((tm, tm), np.float32)), BF16)

    sizes = [GROUP, GROUP, GROUP, N_HEADS] + [GROUP] * 7
    starts = np.cumsum([0] + sizes)
    order = [0, 1, 2, 4, 5, 6, 7, 8, 9, 10]

    x = x.astype(F32)
    for layer in range(depth):
        wl = w_in[layer]
        w_packed = jnp.concatenate(
            [wl[:, starts[k]:starts[k + 1]] for k in order]
            + [wl[:, starts[3]:starts[4]], jnp.zeros((D, LANES - N_HEADS), wl.dtype)], axis=1).astype(BF16)
        bf = jnp.zeros((1, LANES), F32).at[0, :N_HEADS].set(b_forget[layer].astype(F32))
        gains = jnp.stack([
            jnp.tile(fox_q_norm[layer].astype(F32), N_HEADS) * (HEAD_DIM ** -0.5 * LOG2E),
            jnp.tile(fox_k_norm[layer].astype(F32), N_HEADS),
            jnp.tile(dil_q_norm[layer].astype(F32), N_HEADS) * HEAD_DIM ** -0.5,
            jnp.tile(dil_k_norm[layer].astype(F32), N_HEADS),
            jnp.tile(diff_q_norm[layer].astype(F32), 2 * N_HEADS) * (DIFF_HALF ** -0.5 * LOG2E),
            jnp.tile(diff_k_norm[layer].astype(F32), 2 * N_HEADS),
            jnp.zeros((GROUP,), F32), jnp.zeros((GROUP,), F32)])
        (fq, fk, fv, cq1, cq2, ck, cv, dq, dk, dv, su) = _inproj(
            x, norm_mix[layer].astype(F32)[None], w_packed, bf, gains, ones64, ones32, tri, aug, tm)

        fox_o = _fox_attention(fq, fk, fv, tq_fox, tq_fox, tkd_fox)

        lam_init = 0.8 - 0.6 * math.exp(-0.3 * layer)
        lam_p = diff_lambda[layer].astype(F32)
        lam = jnp.exp(jnp.sum(lam_p[0] * lam_p[1])) - jnp.exp(jnp.sum(lam_p[2] * lam_p[3])) + lam_init
        out_gain = jnp.zeros((1, LANES), F32).at[0, :HEAD_DIM].set(diff_out_norm[layer].astype(F32))
        diff_o = _diff_attention(jnp.stack([lam, 1.0 - lam_init]).astype(F32), cq1, cq2, ck, cv, diff_bias,
                                 out_gain, t_diff)

        dil_o = _dilated_attention(dq, dk, dv, dil_bias)

        s5p = _s5_params(s5_a_re[layer], s5_a_im[layer], s5_log_dt[layer], s5_b_re[layer], s5_b_im[layer],
                         s5_c_re[layer], s5_c_im[layer], s5_d[layer])
        s5_o = _s5(su.reshape(S * B, GROUP), *s5p, s5_glu_w[layer].astype(BF16),
                   s5_glu_b[layer].astype(F32)[None], B, tc).reshape(S, B * GROUP)

        wo = w_out[layer]
        x = _outproj(x, fox_o, dil_o, diff_o, s5_o,
                     _pad_head_rows(wo[0:GROUP]), wo[GROUP:2 * GROUP].astype(BF16),
                     _pad_head_rows(wo[2 * GROUP:3 * GROUP]), wo[3 * GROUP:].astype(BF16), tm)

        x2 = x.reshape(T, D)
        h, p1, n1, p2, r2 = _peer_select(x2, norm_ffn[layer].astype(F32)[None], peer_wq[layer].astype(BF16),
                                         peer_keys[layer].astype(BF16), tt_sel)
        x = _peer_dense(x2, h, p1, n1, p2, r2, peer_u[layer].astype(BF16), peer_v[layer].astype(BF16).T,
                        tt_peer, te).reshape(B, S, D)
    return x
```

```python
import functools
import math

import numpy as np
import jax
import jax.numpy as jnp
from jax import lax
from jax.experimental import pallas as pl
from jax.experimental.pallas import tpu as pltpu

F32 = jnp.float32
BF16 = jnp.bfloat16

LANES = 128
GROUP = 256
HEAD_DIM = 64
N_HEADS = GROUP // HEAD_DIM
DIFF_HALF = HEAD_DIM // 2
S5_CH = 16
S5_GROUPS = GROUP // S5_CH
S5_STATE = 64
S5_WIDE = S5_GROUPS * S5_STATE
DIL_BLOCK = 128
DIL_PAIRS = ((128, 1), (512, 4), (2048, 16))
DIL_SUPER = DIL_BLOCK * DIL_PAIRS[-1][1]
REL_BUCKETS = 32
REL_MAX_EXACT = REL_BUCKETS // 2
REL_MAX_DIST = 2048
N_KEYS = 128
PEER_HEADS = 8
PEER_TOPK = 16
RMS_EPS = 1e-6
NEG = -1e30
LOG2E = math.log2(math.e)
GELU_A = -2.0 * math.sqrt(2.0 / math.pi) * LOG2E
GELU_B = GELU_A * 0.044715
PEER_SUB = 512
VMEM_LIMIT = 56 * 1024 * 1024

IN_COLS = 10 * GROUP + LANES
FORGET_COL = 10 * GROUP


def _cparams(sem):
    return pltpu.CompilerParams(dimension_semantics=sem, vmem_limit_bytes=VMEM_LIMIT)


def _dot(a, b):
    return jnp.dot(a, b, preferred_element_type=F32)


def _dot_nt(a, b):
    return lax.dot_general(a, b, (((1,), (1,)), ((), ())), preferred_element_type=F32)


def _split3(x):
    hi = x.astype(BF16)
    r = x - hi.astype(F32)
    mid = r.astype(BF16)
    lo = (r - mid.astype(F32)).astype(BF16)
    return hi, mid, lo


def _gelu(x):
    return 0.5 * x * (1.0 + jnp.tanh(math.sqrt(2.0 / math.pi) * (x + 0.044715 * (x * x * x))))


def _t5_bucket(dist):
    n = jnp.maximum(dist, 0)
    nf = jnp.maximum(n, REL_MAX_EXACT).astype(F32)
    large = REL_MAX_EXACT + (jnp.log(nf / REL_MAX_EXACT) / math.log(REL_MAX_DIST / REL_MAX_EXACT)
                             * (REL_BUCKETS - REL_MAX_EXACT)).astype(jnp.int32)
    large = jnp.minimum(large, REL_BUCKETS - 1)
    return jnp.where(n < REL_MAX_EXACT, n, large)


def _bucket_saturation():
    n = REL_MAX_EXACT * (REL_MAX_DIST / REL_MAX_EXACT) ** ((REL_BUCKETS - 1 - REL_MAX_EXACT)
                                                            / (REL_BUCKETS - REL_MAX_EXACT))
    return int(math.ceil(n)) + 2


def _inproj_kernel(x_ref, g_ref, w_ref, bf_ref, gains_ref, ones64_ref, ones32_ref, tri_ref, aug_ref,
                   fq_ref, fk_ref, fv_ref, cq1_ref, cq2_ref, ck_ref, cv_ref,
                   dq_ref, dk_ref, dv_ref, su_ref, carry_ref):
    @pl.when(pl.program_id(1) == 0)
    def _():
        carry_ref[...] = jnp.zeros_like(carry_ref)

    x = x_ref[0]
    tm = x.shape[0]
    h = x * lax.rsqrt(jnp.mean(x * x, axis=-1, keepdims=True) + RMS_EPS) * g_ref[...]
    z = _dot(h.astype(BF16), w_ref[...])

    def group(idx):
        return z[:, idx * GROUP:(idx + 1) * GROUP]

    def head_norm(zg, ones_ref, width, gain_row):
        ss = _dot((zg * zg).astype(BF16), ones_ref[...])
        return zg * lax.rsqrt(ss * (1.0 / width) + RMS_EPS) * gains_ref[gain_row:gain_row + 1, :]

    lane = lax.broadcasted_iota(jnp.int32, (tm, LANES), 1)

    def head_chunk(zg, hh):
        c = zg[:, (hh // 2) * LANES:(hh // 2 + 1) * LANES]
        if hh % 2:
            c = pltpu.roll(c, HEAD_DIM, axis=1)
        return c

    f = z[:, FORGET_COL:FORGET_COL + LANES] + bf_ref[...]
    ls = jnp.minimum(f, 0.0) - jnp.log1p(jnp.exp(-jnp.abs(f)))
    p0, p1, p2 = _split3(ls)
    tri = tri_ref[...]
    cum = (_dot(tri, p0) + _dot(tri, p1)) + _dot(tri, p2) + carry_ref[...]
    carry_ref[...] = cum[tm - 1:tm, :]

    fq = head_norm(group(0), ones64_ref, HEAD_DIM, 0)
    fk = head_norm(group(1), ones64_ref, HEAD_DIM, 1)
    fv = group(2)
    dq = head_norm(group(3), ones64_ref, HEAD_DIM, 2)
    dk = head_norm(group(4), ones64_ref, HEAD_DIM, 3)
    cq = head_norm(group(6), ones32_ref, DIFF_HALF, 4)
    ck = head_norm(group(7), ones32_ref, DIFF_HALF, 5)
    cv = group(8)

    for hf in range(2):
        dq_ref[0, hf] = dq[:, hf * LANES:(hf + 1) * LANES]
        dk_ref[0, hf] = dk[:, hf * LANES:(hf + 1) * LANES]
        dv_ref[0, hf] = group(5)[:, hf * LANES:(hf + 1) * LANES]
    su_ref[...] = group(9)

    one = jnp.ones((tm, LANES), F32)
    zero = jnp.zeros((tm, LANES), F32)
    for hh in range(N_HEADS):
        c = jnp.broadcast_to(cum[:, hh:hh + 1], (tm, LANES)) * LOG2E
        c0, c1, c2 = _split3(c)
        c0, c1, c2 = c0.astype(F32), c1.astype(F32), c2.astype(F32)
        qa = jnp.where(lane == 64, c0, jnp.where(lane == 65, c1, jnp.where(lane == 66, c2,
             jnp.where(lane < 70, one, zero))))
        ka = jnp.where(lane == 67, -c0, jnp.where(lane == 68, -c1, jnp.where(lane == 69, -c2,
             jnp.where(lane < 70, one, zero))))
        fq_ref[0, hh] = jnp.where(lane < HEAD_DIM, head_chunk(fq, hh), qa).astype(BF16)
        fk_ref[0, hh] = jnp.where(lane < HEAD_DIM, head_chunk(fk, hh), ka).astype(BF16)
        vaug = aug_ref[0:1, :]
        fv_ref[0, hh] = jnp.where(lane < HEAD_DIM, head_chunk(fv, hh), vaug).astype(BF16)
        qaug = aug_ref[1:2, :]
        cqc = head_chunk(cq, hh)
        cq1_ref[0, hh] = jnp.where(lane < DIFF_HALF, cqc, qaug).astype(BF16)
        cq2_ref[0, hh] = jnp.where((lane >= DIFF_HALF) & (lane < HEAD_DIM), cqc, qaug).astype(BF16)
        kaug = aug_ref[2 + hh:3 + hh, :]
        ck_ref[0, hh] = jnp.where(lane < HEAD_DIM, head_chunk(ck, hh), kaug).astype(BF16)
        cv_ref[0, hh] = jnp.where(lane < HEAD_DIM, head_chunk(cv, hh), vaug).astype(BF16)


def _inproj(x, g, w, bf, gains, ones64, ones32, tri, aug, tm):
    B, S, D = x.shape
    nt = S // tm
    hm = jax.ShapeDtypeStruct((B, N_HEADS, S, LANES), BF16)
    nat = jax.ShapeDtypeStruct((B, 2, S, LANES), F32)
    hm_spec = pl.BlockSpec((1, N_HEADS, tm, LANES), lambda b, i: (b, 0, i, 0))
    nat_spec = pl.BlockSpec((1, 2, tm, LANES), lambda b, i: (b, 0, i, 0))

    def full(a):
        return pl.BlockSpec(a.shape, lambda b, i: (0,) * a.ndim)

    return pl.pallas_call(
        _inproj_kernel,
        grid=(B, nt),
        in_specs=[pl.BlockSpec((1, tm, D), lambda b, i: (b, i, 0)),
                  full(g), full(w), full(bf), full(gains), full(ones64), full(ones32), full(tri), full(aug)],
        out_specs=[hm_spec] * 7 + [nat_spec] * 3 + [pl.BlockSpec((tm, GROUP), lambda b, i: (i, b))],
        out_shape=[hm] * 7 + [nat] * 3 + [jax.ShapeDtypeStruct((S, B * GROUP), F32)],
        scratch_shapes=[pltpu.VMEM((1, LANES), F32)],
        compiler_params=_cparams(("arbitrary", "arbitrary")),
        name="inproj",
    )(x, g, w, bf, gains, ones64, ones32, tri, aug)


def _flash_step(q, k, v, m, acc, bias=None):
    s = _dot_nt(q, k)
    if bias is not None:
        s = s + bias
    m_new = jnp.maximum(m, jnp.max(s, axis=1, keepdims=True))
    alpha = jnp.exp2(m - m_new)
    p = jnp.exp2(s - m_new)
    acc = alpha * acc + _dot(p.astype(BF16), v)
    return m_new, acc


def _fox_kernel(q_ref, k_ref, v_ref, o_ref, *, tq, tk, tkd):
    i = pl.program_id(2)
    q = q_ref[0, 0]

    def kv(off, size):
        return k_ref[0, 0, pl.ds(off, size), :], v_ref[0, 0, pl.ds(off, size), :]

    def body(j, carry):
        m, acc = carry
        k, v = kv(pl.multiple_of(j * tk, tk), tk)
        return _flash_step(q, k, v, m, acc)

    m0 = jnp.full((tq, 1), NEG, F32)
    acc0 = jnp.zeros((tq, LANES), F32)
    m, acc = lax.fori_loop(0, i * (tq // tk), body, (m0, acc0))
    for d in range(tq // tkd):
        rows = tq - d * tkd
        k, v = kv(pl.multiple_of(i * tq + d * tkd, tkd), tkd)
        row = lax.broadcasted_iota(jnp.int32, (rows, tkd), 0)
        col = lax.broadcasted_iota(jnp.int32, (rows, tkd), 1)
        mask = jnp.where(col <= row, 0.0, NEG)
        m_d, acc_d = _flash_step(q[d * tkd:], k, v, m[d * tkd:], acc[d * tkd:], mask)
        if d:
            m = jnp.concatenate([m[:d * tkd], m_d], axis=0)
            acc = jnp.concatenate([acc[:d * tkd], acc_d], axis=0)
        else:
            m, acc = m_d, acc_d
    den = jnp.broadcast_to(acc[:, HEAD_DIM:HEAD_DIM + 1], (tq, LANES))
    o_ref[0, 0] = (acc / den).astype(BF16)


def _fox_attention(q, k, v, tq, tk, tkd):
    B, H, S, _ = q.shape
    return pl.pallas_call(
        functools.partial(_fox_kernel, tq=tq, tk=tk, tkd=tkd),
        grid=(B, H, S // tq),
        in_specs=[pl.BlockSpec((1, 1, tq, LANES), lambda b, h, i: (b, h, i, 0)),
                  pl.BlockSpec((1, 1, S, LANES), lambda b, h, i: (b, h, 0, 0)),
                  pl.BlockSpec((1, 1, S, LANES), lambda b, h, i: (b, h, 0, 0))],
        out_specs=pl.BlockSpec((1, 1, tq, LANES), lambda b, h, i: (b, h, i, 0)),
        out_shape=jax.ShapeDtypeStruct((B, H, S, LANES), BF16),
        compiler_params=_cparams(("arbitrary", "arbitrary", "arbitrary")),
        name="fox_attention",
    )(q, k, v)


def _diff_kernel(lam_ref, q1_ref, q2_ref, k_ref, v_ref, bias_ref, gain_ref, o_ref, *, t, n_near):
    i = pl.program_id(2)
    q = jnp.concatenate([q1_ref[0, 0], q2_ref[0, 0]], axis=0)
    c = jnp.minimum(i + 1, n_near)
    n_far = i + 1 - c
    far_pairs = n_far // 2
    near_single = c % 2

    def step(carry, tile, tiles, bias=None):
        m, acc = carry
        off = pl.multiple_of(tile * t, t)
        k, v = k_ref[0, 0, pl.ds(off, tiles * t), :], v_ref[0, 0, pl.ds(off, tiles * t), :]
        if bias is not None:
            bias = jnp.concatenate([bias, bias], axis=0)
        return _flash_step(q, k, v, m, acc, bias)

    def near_pair(p, carry):
        first = c - 1 - near_single - 2 * p
        return step(carry, i - first, 2, bias_ref[0, (first - 1) // 2])

    m0 = jnp.full((2 * t, 1), NEG, F32)
    a0 = jnp.zeros((2 * t, LANES), F32)
    carry = lax.fori_loop(0, far_pairs, lambda j, cr: step(cr, 2 * j, 2), (m0, a0))
    carry = lax.fori_loop(0, n_far % 2, lambda _, cr: step(cr, n_far - 1, 1), carry)
    carry = lax.fori_loop(0, near_single, lambda _, cr: step(cr, n_far, 1, bias_ref[0, (c - 1) // 2, :, t:]), carry)
    _, acc = lax.fori_loop(0, c // 2, near_pair, carry)
    a1, a2 = acc[:t], acc[t:]
    d1 = jnp.broadcast_to(a1[:, HEAD_DIM:HEAD_DIM + 1], (t, LANES))
    d2 = jnp.broadcast_to(a2[:, HEAD_DIM:HEAD_DIM + 1], (t, LANES))
    lane = lax.broadcasted_iota(jnp.int32, (t, LANES), 1)
    o = jnp.where(lane < HEAD_DIM, a1 / d1 - lam_ref[0] * (a2 / d2), 0.0)
    ms = jnp.sum(o * o, axis=1, keepdims=True) * (1.0 / HEAD_DIM)
    o_ref[0, 0] = (o * lax.rsqrt(ms + RMS_EPS) * gain_ref[...] * lam_ref[1]).astype(BF16)


def _diff_attention(lam, q1, q2, k, v, bias, gain, t):
    B, H, S, _ = q1.shape
    n_near = 2 * bias.shape[1]
    qspec = pl.BlockSpec((1, 1, t, LANES), lambda b, h, i: (b, h, i, 0))
    kspec = pl.BlockSpec((1, 1, S, LANES), lambda b, h, i: (b, h, 0, 0))
    return pl.pallas_call(
        functools.partial(_diff_kernel, t=t, n_near=n_near),
        grid=(B, H, S // t),
        in_specs=[pl.BlockSpec(memory_space=pltpu.SMEM), qspec, qspec, kspec, kspec,
                  pl.BlockSpec((1, n_near // 2, t, 2 * t), lambda b, h, i: (h, 0, 0, 0)),
                  pl.BlockSpec((1, LANES), lambda b, h, i: (0, 0))],
        out_specs=qspec,
        out_shape=jax.ShapeDtypeStruct((B, H, S, LANES), BF16),
        compiler_params=_cparams(("arbitrary", "arbitrary", "arbitrary")),
        name="diff_attention",
    )(lam, q1, q2, k, v, bias, gain)


def _dil_kernel(qc_ref, kp_ref, kc_ref, vp_ref, vc_ref, b0_ref, b1_ref, b2_ref, o_ref, num_ref, st_ref):
    first = pl.program_id(1) == 0
    w = DIL_BLOCK
    lane = lax.broadcasted_iota(jnp.int32, (w, GROUP), 1)
    lane_s = lax.broadcasted_iota(jnp.int32, (w, LANES), 1)
    heads = [(lane >= hh * HEAD_DIM) & (lane < (hh + 1) * HEAD_DIM) for hh in range(N_HEADS)]
    ones = jnp.ones((w, LANES), BF16)

    def rows(cur_ref, prev_ref, off, res, dil):
        ref, base = (cur_ref, off) if off >= 0 else (prev_ref, off + DIL_SUPER)
        if dil == 1:
            parts = [ref[0, hf, base:base + w, :] for hf in range(2)]
        else:
            parts = [ref[0, hf, pl.ds(base + res, w, stride=dil), :] for hf in range(2)]
        return jnp.concatenate(parts, axis=1).astype(BF16)

    for n, (bias_ref, (_, dil)) in enumerate(zip((b0_ref, b1_ref, b2_ref), DIL_PAIRS)):
        span = w * dil
        for u in range(DIL_SUPER // span):
            off = u * span
            for res in range(dil):
                q = rows(qc_ref, qc_ref, off, res, dil)
                kc, vc = rows(kc_ref, kp_ref, off, res, dil), rows(vc_ref, vp_ref, off, res, dil)
                kp, vp = rows(kc_ref, kp_ref, off - span, res, dil), rows(vc_ref, vp_ref, off - span, res, dil)
                qs = jnp.concatenate([jnp.where(heads[hh], q, jnp.zeros_like(q)) for hh in range(N_HEADS)], axis=0)
                sp = _dot_nt(qs, kp) + bias_ref[:, 0:w]
                if u == 0:
                    sp = sp + jnp.where(first, NEG, 0.0)
                sc = _dot_nt(qs, kc) + bias_ref[:, w:2 * w]
                m = jnp.maximum(jnp.max(sp, axis=1, keepdims=True), jnp.max(sc, axis=1, keepdims=True))
                pp = jnp.exp(sp - m).astype(BF16)
                pc = jnp.exp(sc - m).astype(BF16)
                den = _dot(pp, ones) + _dot(pc, ones)
                o = _dot(pp, vp) + _dot(pc, vc)
                num = jnp.zeros((w, GROUP), F32)
                st = jnp.zeros((w, LANES), F32)
                for hh in range(N_HEADS):
                    rs = slice(hh * w, (hh + 1) * w)
                    num = jnp.where(heads[hh], o[rs], num)
                    st = jnp.where(lane_s == hh, den[rs], jnp.where(lane_s == N_HEADS + hh, m[rs], st))
                dst = slice(off, off + w) if dil == 1 else pl.ds(off + res, w, stride=dil)
                for hf in range(2):
                    num_ref[n, hf, dst, :] = num[:, hf * LANES:(hf + 1) * LANES]
                st_ref[n, dst, :] = st

    tm = 2 * w
    lane2 = lax.broadcasted_iota(jnp.int32, (tm, GROUP), 1)
    for c in range(DIL_SUPER // tm):
        sl = slice(c * tm, (c + 1) * tm)
        nums = [jnp.concatenate([num_ref[n, 0, sl, :], num_ref[n, 1, sl, :]], axis=1) for n in range(3)]
        sts = [st_ref[n, sl, :] for n in range(3)]
        out = jnp.zeros((tm, GROUP), F32)
        for hh in range(N_HEADS):
            ms = [s[:, N_HEADS + hh:N_HEADS + hh + 1] for s in sts]
            ds = [s[:, hh:hh + 1] for s in sts]
            mx = jnp.maximum(jnp.maximum(ms[0], ms[1]), ms[2])
            ws = [jnp.exp(m - mx) for m in ms]
            den = ws[0] * ds[0] + ws[1] * ds[1] + ws[2] * ds[2]
            o = (ws[0] * nums[0] + ws[1] * nums[1] + ws[2] * nums[2]) / den
            out = jnp.where((lane2 >= hh * HEAD_DIM) & (lane2 < (hh + 1) * HEAD_DIM), o, out)
        o_ref[0, sl, :] = out.astype(BF16)


def _dilated_attention(q, k, v, biases):
    B, _, S, _ = q.shape
    cur = pl.BlockSpec((1, 2, DIL_SUPER, LANES), lambda b, n: (b, 0, n, 0))
    prev = pl.BlockSpec((1, 2, DIL_SUPER, LANES), lambda b, n: (b, 0, jnp.maximum(n - 1, 0), 0))
    bspec = pl.BlockSpec(biases[0].shape, lambda b, n: (0, 0))
    return pl.pallas_call(
        _dil_kernel,
        grid=(B, S // DIL_SUPER),
        in_specs=[cur, prev, cur, prev, cur, bspec, bspec, bspec],
        out_specs=pl.BlockSpec((1, DIL_SUPER, GROUP), lambda b, n: (b, n, 0)),
        out_shape=jax.ShapeDtypeStruct((B, S, GROUP), BF16),
        scratch_shapes=[pltpu.VMEM((3, 2, DIL_SUPER, LANES), F32), pltpu.VMEM((3, DIL_SUPER, LANES), F32)],
        compiler_params=_cparams(("arbitrary", "arbitrary")),
        name="dilated_attention",
    )(q, k, k, v, v, *biases)


def _s5_kernel(u_ref, are_ref, aim_ref, bre_ref, bim_ref, cre_ref, cim_ref, d_ref, gw_ref, gb_ref,
               o_ref, xr_ref, xi_ref, bur_ref, bui_ref, sr_ref, si_ref, *, nb, tc):
    @pl.when(pl.program_id(0) == 0)
    def _():
        sr_ref[...] = jnp.zeros_like(sr_ref)
        si_ref[...] = jnp.zeros_like(si_ref)

    u = u_ref[...]
    ub = u.astype(BF16)
    bur_ref[...] = _dot(ub, bre_ref[...])
    bui_ref[...] = _dot(ub, bim_ref[...])
    ar = jnp.broadcast_to(are_ref[...], (nb, S5_WIDE))
    ai = jnp.broadcast_to(aim_ref[...], (nb, S5_WIDE))

    def step(t, carry):
        xr, xi = carry
        rows = pl.ds(pl.multiple_of(t * nb, nb), nb)
        nr = ar * xr - ai * xi + bur_ref[rows, :]
        ni = ar * xi + ai * xr + bui_ref[rows, :]
        xr_ref[rows, :] = nr
        xi_ref[rows, :] = ni
        return nr, ni

    xr, xi = lax.fori_loop(0, tc, step, (sr_ref[...], si_ref[...]))
    sr_ref[...] = xr
    si_ref[...] = xi
    y = (_dot(xr_ref[...].astype(BF16), cre_ref[...]) - _dot(xi_ref[...].astype(BF16), cim_ref[...])
         + d_ref[...] * u)
    g = _gelu(y)
    gate = _dot(g.astype(BF16), gw_ref[...]) + gb_ref[...]
    o_ref[...] = (g * (1.0 / (1.0 + jnp.exp(-gate)))).astype(BF16)


def _s5(u2, are, aim, bre, bim, cre, cim, d, gw, gb, nb, tc):
    rows = u2.shape[0]
    tr = tc * nb

    def full(a):
        return pl.BlockSpec(a.shape, lambda i: (0,) * a.ndim)

    consts = (are, aim, bre, bim, cre, cim, d, gw, gb)
    return pl.pallas_call(
        functools.partial(_s5_kernel, nb=nb, tc=tc),
        grid=(rows // tr,),
        in_specs=[pl.BlockSpec((tr, GROUP), lambda i: (i, 0))] + [full(a) for a in consts],
        out_specs=pl.BlockSpec((tr, GROUP), lambda i: (i, 0)),
        out_shape=jax.ShapeDtypeStruct((rows, GROUP), BF16),
        scratch_shapes=[pltpu.VMEM((tr, S5_WIDE), F32)] * 4 + [pltpu.VMEM((nb, S5_WIDE), F32)] * 2,
        compiler_params=_cparams(("arbitrary",)),
        name="s5_mixer",
    )(u2, *consts)


def _outproj_kernel(x_ref, fox_ref, dil_ref, dif_ref, s5_ref, wf_ref, wd_ref, wc_ref, ws_ref, o_ref):
    acc = x_ref[0] + _dot(dil_ref[0], wd_ref[...]) + _dot(s5_ref[...], ws_ref[...])
    for hh in range(N_HEADS):
        acc = acc + _dot(fox_ref[0, hh], wf_ref[hh]) + _dot(dif_ref[0, hh], wc_ref[hh])
    o_ref[0] = acc


def _outproj(x, fox, dil, dif, s5, wf, wd, wc, ws, tm):
    B, S, D = x.shape
    xspec = pl.BlockSpec((1, tm, D), lambda b, i: (b, i, 0))
    hspec = pl.BlockSpec((1, N_HEADS, tm, LANES), lambda b, i: (b, 0, i, 0))

    def full(a):
        return pl.BlockSpec(a.shape, lambda b, i: (0,) * a.ndim)

    return pl.pallas_call(
        _outproj_kernel,
        grid=(B, S // tm),
        in_specs=[xspec, hspec, pl.BlockSpec((1, tm, GROUP), lambda b, i: (b, i, 0)), hspec,
                  pl.BlockSpec((tm, GROUP), lambda b, i: (i, b)),
                  full(wf), full(wd), full(wc), full(ws)],
        out_specs=xspec,
        out_shape=jax.ShapeDtypeStruct((B, S, D), F32),
        compiler_params=_cparams(("arbitrary", "arbitrary")),
        name="outproj",
    )(x, fox, dil, dif, s5, wf, wd, wc, ws)


def _staircase(n):
    return [(a, b) for a in range(n) for b in range(n) if (a + 1) * (b + 1) <= n]


def _peer_select_kernel(x_ref, g_ref, wq_ref, keys_ref, h_ref, p1_ref, n1_ref, p2_ref, r2_ref):
    x = x_ref[...]
    tt = x.shape[0]
    hf = x * lax.rsqrt(jnp.mean(x * x, axis=-1, keepdims=True) + RMS_EPS) * g_ref[...]
    h = hf.astype(BF16)
    h_ref[...] = hf.T.astype(BF16)
    q = _dot(h, wq_ref[...]).astype(BF16)
    n_top = PEER_TOPK + 1
    pairs = _staircase(n_top)
    chains = [(hh, half) for hh in range(PEER_HEADS) for half in range(2)]
    scores = {}
    for hh, half in chains:
        c = (hh * 2 + half) * N_KEYS
        scores[hh, half] = _dot_nt(keys_ref[hh, half], q[:, c:c + N_KEYS])
    half_n = N_KEYS // 2
    swap, hi, lo = {}, {}, {}
    for ch in chains:
        a, b = scores[ch][:half_n], scores[ch][half_n:]
        swap[ch] = b > a
        hi[ch] = jnp.maximum(a, b)
        lo[ch] = jnp.minimum(a, b)
    tops = {ch: [] for ch in chains}
    for kk in range(n_top):
        for ch in chains:
            mx = jnp.max(hi[ch], axis=0, keepdims=True)
            tops[ch].append(mx)
            hit = hi[ch] == mx
            hi[ch] = jnp.where(hit, lo[ch], hi[ch])
            lo[ch] = jnp.where(hit, NEG * (1.0 + kk / 64.0), lo[ch])
    work = {}
    for hh in range(PEER_HEADS):
        ch = (hh, 1)
        both = hi[ch] < 0.5 * NEG
        one = lo[ch] < 0.5 * NEG
        big = jnp.where(both, hi[ch], jnp.where(one, lo[ch], 0.0))
        small = jnp.where(both, lo[ch], 0.0)
        work[ch] = jnp.concatenate([jnp.where(swap[ch], small, big),
                                    jnp.where(swap[ch], big, small)], axis=0)
    cands = [jnp.concatenate([tops[hh, 0][a] + tops[hh, 1][b] for a, b in pairs]
                             + [jnp.full((8 - len(pairs) % 8, tt), NEG, F32)], axis=0) for hh in range(PEER_HEADS)]
    bests = [[] for _ in range(PEER_HEADS)]
    for _ in range(n_top):
        for hh in range(PEER_HEADS):
            mx = jnp.max(cands[hh], axis=0, keepdims=True)
            bests[hh].append(mx)
            cands[hh] = jnp.where(cands[hh] == mx, NEG, cands[hh])
    for hh in range(PEER_HEADS):
        best = bests[hh]
        tops_h = [tops[hh, 0], tops[hh, 1]]
        scores_h = [scores[hh, 0], scores[hh, 1]]
        left = work[hh, 1]
        rank = jnp.where(left < 0.5 * NEG, (left * (1.0 / NEG) - 1.0) * 64.0, float(N_KEYS))
        z = jnp.zeros((1, tt), F32)
        for kk in range(PEER_TOPK):
            z = z + jnp.exp(best[kk] - best[0])
        cut = 0.5 * (best[PEER_TOPK - 1] + best[PEER_TOPK])
        n1 = jnp.zeros((N_KEYS, tt), F32)
        for b in range(PEER_TOPK):
            n1 = jnp.where(scores_h[0] >= cut - tops_h[1][b], b + 1.0, n1)
        p1_ref[hh] = jnp.exp(scores_h[0] - tops_h[0][0]) * (1.0 / z)
        n1_ref[hh] = n1
        p2_ref[hh] = pltpu.bitcast(jnp.exp(scores_h[1] - tops_h[1][0]).astype(BF16), jnp.uint32)
        r2_ref[hh] = pltpu.bitcast(rank.astype(BF16), jnp.uint32)


def _peer_select(x2, g, wq, keys, tt):
    T, D = x2.shape
    rows = pl.BlockSpec((PEER_HEADS, N_KEYS, tt), lambda i: (0, 0, i))
    words = pl.BlockSpec((PEER_HEADS, N_KEYS // 2, tt), lambda i: (0, 0, i))
    return pl.pallas_call(
        _peer_select_kernel,
        grid=(T // tt,),
        in_specs=[pl.BlockSpec((tt, D), lambda i: (i, 0)),
                  pl.BlockSpec(g.shape, lambda i: (0, 0)),
                  pl.BlockSpec(wq.shape, lambda i: (0, 0)),
                  pl.BlockSpec(keys.shape, lambda i: (0, 0, 0, 0))],
        out_specs=[pl.BlockSpec((D, tt), lambda i: (0, i)), rows, rows, words, words],
        out_shape=[jax.ShapeDtypeStruct((D, T), BF16),
                   jax.ShapeDtypeStruct((PEER_HEADS, N_KEYS, T), F32),
                   jax.ShapeDtypeStruct((PEER_HEADS, N_KEYS, T), F32),
                   jax.ShapeDtypeStruct((PEER_HEADS, N_KEYS // 2, T), jnp.uint32),
                   jax.ShapeDtypeStruct((PEER_HEADS, N_KEYS // 2, T), jnp.uint32)],
        compiler_params=_cparams(("arbitrary",)),
        name="peer_select",
    )(x2, g, wq, keys)


def _gelu_gate(x, g):
    ga, gb, one = (jnp.asarray(c, x.dtype) for c in (GELU_A, GELU_B, 1.0))
    e = jnp.exp2(x * (ga + gb * (x * x)))
    return x * g / (one + e)


def _peer_dense_kernel(x_ref, h_ref, p1_ref, n1_ref, p2_ref, r2_ref, u_ref, vt_ref, o_ref,
                       a_ref, w_ref, acc_ref, *, te, tt, n_e):
    s = pl.program_id(0)

    @pl.when(s == 0)
    def _():
        a_ref[...] = jnp.zeros_like(a_ref)
        w_ref[...] = jnp.zeros_like(w_ref)
        acc_ref[...] = jnp.zeros_like(acc_ref)

    slot_a = s % 2
    slot_b = 1 - slot_a
    j_b = jnp.maximum(s - 1, 0) % n_e
    j_c = jnp.maximum(s - 2, 0) % n_e

    h = h_ref[...]
    for q in range(te // PEER_SUB):
        a_ref[slot_a, q * PEER_SUB:(q + 1) * PEER_SUB, :] = _dot(
            u_ref[q * PEER_SUB:(q + 1) * PEER_SUB, :], h).astype(BF16)

    acc_ref[...] = jnp.where(j_c == 0, 0.0, acc_ref[...]) + _dot(vt_ref[...], w_ref[slot_a])

    rows_per_step = te // N_KEYS
    rows = pl.ds(pl.multiple_of(j_b * rows_per_step, rows_per_step), rows_per_step)
    for r in range(rows_per_step):
        for c in range(tt // LANES):
            cols = slice(c * LANES, (c + 1) * LANES)
            g = jnp.zeros((N_KEYS, LANES), BF16)
            for hh in range(PEER_HEADS):
                gate = jnp.broadcast_to(p1_ref[hh, rows, cols][r:r + 1], (N_KEYS, LANES)).astype(BF16)
                count = jnp.broadcast_to(n1_ref[hh, rows, cols][r:r + 1], (N_KEYS, LANES)).astype(BF16)
                r2 = pltpu.bitcast(r2_ref[hh, :, cols], BF16)
                p2 = pltpu.bitcast(p2_ref[hh, :, cols], BF16)
                sel = jnp.where(r2 < count, p2, jnp.zeros((), BF16))
                g = g + sel * gate
            a = a_ref[slot_b, r * N_KEYS:(r + 1) * N_KEYS, cols]
            w_ref[slot_b, r * N_KEYS:(r + 1) * N_KEYS, cols] = _gelu_gate(a, g)

    @pl.when((j_c == n_e - 1) & (s >= 2))
    def _():
        o_ref[...] = x_ref[...] + acc_ref[...].T


def _peer_dense(x2, h, p1, n1, p2, r2, u, vt, tt, te):
    T, D = x2.shape
    n_e = u.shape[0] // te
    last = (T // tt) * n_e - 1

    def tok(lag):
        return lambda s: jnp.clip(s - lag, 0, last) // n_e

    def tile(lag):
        return lambda s: jnp.clip(s - lag, 0, last) % n_e

    rows = pl.BlockSpec((PEER_HEADS, N_KEYS, tt), lambda s: (0, 0, tok(1)(s)))
    words = pl.BlockSpec((PEER_HEADS, N_KEYS // 2, tt), lambda s: (0, 0, tok(1)(s)))
    return pl.pallas_call(
        functools.partial(_peer_dense_kernel, te=te, tt=tt, n_e=n_e),
        grid=(last + 3,),
        in_specs=[pl.BlockSpec((tt, D), lambda s: (tok(2)(s), 0)),
                  pl.BlockSpec((D, tt), lambda s: (0, tok(0)(s))),
                  rows, rows, words, words,
                  pl.BlockSpec((te, D), lambda s: (tile(0)(s), 0)),
                  pl.BlockSpec((D, te), lambda s: (0, tile(2)(s)))],
        out_specs=pl.BlockSpec((tt, D), lambda s: (tok(2)(s), 0)),
        out_shape=jax.ShapeDtypeStruct((T, D), F32),
        scratch_shapes=[pltpu.VMEM((2, te, tt), BF16), pltpu.VMEM((2, te, tt), BF16), pltpu.VMEM((D, tt), F32)],
        compiler_params=_cparams(("arbitrary",)),
        name="peer_dense",
    )(x2, h, p1, n1, p2, r2, u, vt)


def _block_ones(width):
    idx = np.arange(GROUP) // width
    return jnp.asarray(idx[:, None] == idx[None, :], BF16)


def _pad_head_rows(w):
    w = w.reshape(N_HEADS, HEAD_DIM, -1)
    return jnp.concatenate([w, jnp.zeros_like(w)], axis=1).astype(BF16)


def _dil_bias(rel_bias, window, dil):
    ws = window // dil
    j = jnp.arange(DIL_BLOCK)[:, None]
    m = jnp.arange(2 * DIL_BLOCK)[None, :]
    sub = DIL_BLOCK + j - m
    band = (sub >= 0) & (sub <= ws)
    bias = rel_bias[_t5_bucket(dil * jnp.clip(sub, 0, ws))].transpose(2, 0, 1).astype(F32)
    return jnp.where(band[None], bias, NEG).reshape(N_HEADS * DIL_BLOCK, 2 * DIL_BLOCK)


def _diff_bias(rel_bias, t, n_near):
    dist = jnp.arange(-(t - 1), n_near * t + 2)
    b = rel_bias[_t5_bucket(dist)].astype(F32) - rel_bias[REL_BUCKETS - 1].astype(F32)
    b = jnp.where((dist >= 0)[:, None], b, NEG).T
    rev = b[:, ::-1]
    n = b.shape[1]
    tiles = []
    for delta in range(n_near):
        start = n - 1 - (delta * t + 2 * t - 2)
        u = jnp.pad(rev[:, start:start + 2 * t - 1], ((0, 0), (0, 1)))
        flat = jnp.tile(u, (1, t))[:, t - 1:t - 1 + t * (2 * t - 1)]
        tiles.append(flat.reshape(-1, t, 2 * t - 1)[:, :, :t])
    pairs = [jnp.concatenate([tiles[2 * p + 1], tiles[2 * p]], axis=-1) for p in range(n_near // 2)]
    return jnp.stack(pairs, axis=1)


def _s5_params(a_re, a_im, log_dt, b_re, b_im, c_re, c_im, d_skip):
    G, N, C = S5_GROUPS, S5_STATE, S5_CH
    dt = jnp.exp(log_dt.astype(F32))[:, None]
    ar, ai = a_re.astype(F32), a_im.astype(F32)
    mag = jnp.exp(dt * ar)
    abar_re, abar_im = mag * jnp.cos(dt * ai), mag * jnp.sin(dt * ai)
    den = ar * ar + ai * ai
    nr, ni = abar_re - 1.0, abar_im
    coef_re = (nr * ar + ni * ai) / den
    coef_im = (ni * ar - nr * ai) / den
    br, bi = b_re.astype(F32), b_im.astype(F32)
    bbar_re = coef_re[..., None] * br - coef_im[..., None] * bi
    bbar_im = coef_re[..., None] * bi + coef_im[..., None] * br
    eye = jnp.eye(G, dtype=F32)

    def b_blockdiag(bb):
        return jnp.einsum('gnc,gk->gckn', bb, eye).reshape(G * C, G * N).astype(BF16)

    def c_blockdiag(cc):
        return jnp.einsum('gcn,gk->gnkc', cc.astype(F32), eye).reshape(G * N, G * C).astype(BF16)

    return (abar_re.reshape(1, G * N), abar_im.reshape(1, G * N), b_blockdiag(bbar_re), b_blockdiag(bbar_im),
            c_blockdiag(c_re), c_blockdiag(c_im), d_skip.astype(F32).reshape(1, G * C))


def _pick(n, pref):
    for c in pref:
        if n % c == 0:
            return c
    raise ValueError("unsupported size %d" % n)


def kernel(x, w_in, b_forget, w_out, norm_mix, norm_ffn, fox_q_norm, fox_k_norm, dil_q_norm, dil_k_norm, diff_q_norm, diff_k_norm, diff_lambda, diff_out_norm, rel_bias, s5_a_re, s5_a_im, s5_log_dt, s5_b_re, s5_b_im, s5_c_re, s5_c_im, s5_d, s5_glu_w, s5_glu_b, peer_wq, peer_keys, peer_u, peer_v):
    B, S, D = x.shape
    depth = w_in.shape[0]
    T = B * S
    assert S % (DIL_BLOCK * DIL_PAIRS[-1][1]) == 0 and D % LANES == 0
    tm = _pick(S, (512, 256, 128))
    t_diff = _pick(S, (512, 256))
    tq_fox = _pick(S, (1024, 512, 256))
    tkd_fox = min(tq_fox, 512)
    tc = _pick(S, (128, 64))
    tt_sel = _pick(T, (512, 256))
    tt_peer = _pick(T, (512, 256))
    te = 2048

    n_sat = _bucket_saturation()
    n_near = -(-(n_sat + t_diff - 1) // t_diff)
    n_near = min(n_near + n_near % 2, S // t_diff)
    rel_bias = rel_bias.astype(F32)
    dil_bias = [_dil_bias(rel_bias[:, :N_HEADS], w, r) for w, r in DIL_PAIRS]
    diff_bias = _diff_bias(rel_bias[:, N_HEADS:], t_diff, n_near) * LOG2E
    far = rel_bias[REL_BUCKETS - 1, N_HEADS:] * LOG2E
    far_hi = far.astype(BF16).astype(F32)
    lane = jnp.arange(LANES)
    aug = jnp.zeros((2 + N_HEADS, LANES), F32)
    aug = aug.at[0].set(jnp.where(lane == HEAD_DIM, 1.0, 0.0))
    aug = aug.at[1].set(jnp.where((lane == HEAD_DIM) | (lane == HEAD_DIM + 1), 1.0, 0.0))
    aug = aug.at[2:].set(jnp.where(lane[None] == HEAD_DIM, far_hi[:, None],
                                   jnp.where(lane[None] == HEAD_DIM + 1, (far - far_hi)[:, None], 0.0)))
    ones64, ones32 = _block_ones(HEAD_DIM), _block_ones(DIFF_HALF)
    tri = jnp.asarray(np.tril(np.ones((tm, tm), np.float32)), BF16)

    sizes = [GROUP, GROUP, GROUP, N_HEADS] + [GROUP] * 7
    starts = np.cumsum([0] + sizes)
    order = [0, 1, 2, 4, 5, 6, 7, 8, 9, 10]

    x = x.astype(F32)
    for layer in range(depth):
        wl = w_in[layer]
        w_packed = jnp.concatenate(
            [wl[:, starts[k]:starts[k + 1]] for k in order]
            + [wl[:, starts[3]:starts[4]], jnp.zeros((D, LANES - N_HEADS), wl.dtype)], axis=1).astype(BF16)
        bf = jnp.zeros((1, LANES), F32).at[0, :N_HEADS].set(b_forget[layer].astype(F32))
        gains = jnp.stack([
            jnp.tile(fox_q_norm[layer].astype(F32), N_HEADS) * (HEAD_DIM ** -0.5 * LOG2E),
            jnp.tile(fox_k_norm[layer].astype(F32), N_HEADS),
            jnp.tile(dil_q_norm[layer].astype(F32), N_HEADS) * HEAD_DIM ** -0.5,
            jnp.tile(dil_k_norm[layer].astype(F32), N_HEADS),
            jnp.tile(diff_q_norm[layer].astype(F32), 2 * N_HEADS) * (DIFF_HALF ** -0.5 * LOG2E),
            jnp.tile(diff_k_norm[layer].astype(F32), 2 * N_HEADS),
            jnp.zeros((GROUP,), F32), jnp.zeros((GROUP,), F32)])
        (fq, fk, fv, cq1, cq2, ck, cv, dq, dk, dv, su) = _inproj(
            x, norm_mix[layer].astype(F32)[None], w_packed, bf, gains, ones64, ones32, tri, aug, tm)

        fox_o = _fox_attention(fq, fk, fv, tq_fox, tq_fox, tkd_fox)

        lam_init = 0.8 - 0.6 * math.exp(-0.3 * layer)
        lam_p = diff_lambda[layer].astype(F32)
        lam = jnp.exp(jnp.sum(lam_p[0] * lam_p[1])) - jnp.exp(jnp.sum(lam_p[2] * lam_p[3])) + lam_init
        out_gain = jnp.zeros((1, LANES), F32).at[0, :HEAD_DIM].set(diff_out_norm[layer].astype(F32))
        diff_o = _diff_attention(jnp.stack([lam, 1.0 - lam_init]).astype(F32), cq1, cq2, ck, cv, diff_bias,
                                 out_gain, t_diff)

        dil_o = _dilated_attention(dq, dk, dv, dil_bias)

        s5p = _s5_params(s5_a_re[layer], s5_a_im[layer], s5_log_dt[layer], s5_b_re[layer], s5_b_im[layer],
                         s5_c_re[layer], s5_c_im[layer], s5_d[layer])
        s5_o = _s5(su.reshape(S * B, GROUP), *s5p, s5_glu_w[layer].astype(BF16),
                   s5_glu_b[layer].astype(F32)[None], B, tc).reshape(S, B * GROUP)

        wo = w_out[layer]
        x = _outproj(x, fox_o, dil_o, diff_o, s5_o,
                     _pad_head_rows(wo[0:GROUP]), wo[GROUP:2 * GROUP].astype(BF16),
                     _pad_head_rows(wo[2 * GROUP:3 * GROUP]), wo[3 * GROUP:].astype(BF16), tm)

        x2 = x.reshape(T, D)
        h, p1, n1, p2, r2 = _peer_select(x2, norm_ffn[layer].astype(F32)[None], peer_wq[layer].astype(BF16),
                                         peer_keys[layer].astype(BF16), tt_sel)
        x = _peer_dense(x2, h, p1, n1, p2, r2, peer_u[layer].astype(BF16), peer_v[layer].astype(BF16).T,
                        tt_peer, te).reshape(B, S, D)
    return x
```

```python
import functools
import math

import numpy as np
import jax
import jax.numpy as jnp
from jax import lax
from jax.experimental import pallas as pl
from jax.experimental.pallas import tpu as pltpu

F32 = jnp.float32
BF16 = jnp.bfloat16

LANES = 128
GROUP = 256
HEAD_DIM = 64
N_HEADS = GROUP // HEAD_DIM
DIFF_HALF = HEAD_DIM // 2
S5_CH = 16
S5_GROUPS = GROUP // S5_CH
S5_STATE = 64
S5_WIDE = S5_GROUPS * S5_STATE
DIL_BLOCK = 128
DIL_PAIRS = ((128, 1), (512, 4), (2048, 16))
DIL_SUPER = DIL_BLOCK * DIL_PAIRS[-1][1]
REL_BUCKETS = 32
REL_MAX_EXACT = REL_BUCKETS // 2
REL_MAX_DIST = 2048
N_KEYS = 128
PEER_HEADS = 8
PEER_TOPK = 16
RMS_EPS = 1e-6
NEG = -1e30
LOG2E = math.log2(math.e)
GELU_A = -2.0 * math.sqrt(2.0 / math.pi) * LOG2E
GELU_B = GELU_A * 0.044715
PEER_SUB = 512
VMEM_LIMIT = 56 * 1024 * 1024

IN_COLS = 10 * GROUP + LANES
FORGET_COL = 10 * GROUP


def _cparams(sem):
    return pltpu.CompilerParams(dimension_semantics=sem, vmem_limit_bytes=VMEM_LIMIT)


def _dot(a, b):
    return jnp.dot(a, b, preferred_element_type=F32)


def _dot_nt(a, b):
    return lax.dot_general(a, b, (((1,), (1,)), ((), ())), preferred_element_type=F32)


def _split3(x):
    hi = x.astype(BF16)
    r = x - hi.astype(F32)
    mid = r.astype(BF16)
    lo = (r - mid.astype(F32)).astype(BF16)
    return hi, mid, lo


def _gelu(x):
    return 0.5 * x * (1.0 + jnp.tanh(math.sqrt(2.0 / math.pi) * (x + 0.044715 * (x * x * x))))


def _t5_bucket(dist):
    n = jnp.maximum(dist, 0)
    nf = jnp.maximum(n, REL_MAX_EXACT).astype(F32)
    large = REL_MAX_EXACT + (jnp.log(nf / REL_MAX_EXACT) / math.log(REL_MAX_DIST / REL_MAX_EXACT)
                             * (REL_BUCKETS - REL_MAX_EXACT)).astype(jnp.int32)
    large = jnp.minimum(large, REL_BUCKETS - 1)
    return jnp.where(n < REL_MAX_EXACT, n, large)


def _bucket_saturation():
    n = REL_MAX_EXACT * (REL_MAX_DIST / REL_MAX_EXACT) ** ((REL_BUCKETS - 1 - REL_MAX_EXACT)
                                                            / (REL_BUCKETS - REL_MAX_EXACT))
    return int(math.ceil(n)) + 2


def _inproj_kernel(x_ref, g_ref, w_ref, bf_ref, gains_ref, ones64_ref, ones32_ref, tri_ref, aug_ref,
                   fq_ref, fk_ref, fv_ref, cq1_ref, cq2_ref, ck_ref, cv_ref,
                   dq_ref, dk_ref, dv_ref, su_ref, carry_ref):
    @pl.when(pl.program_id(1) == 0)
    def _():
        carry_ref[...] = jnp.zeros_like(carry_ref)

    x = x_ref[0]
    tm = x.shape[0]
    h = x * lax.rsqrt(jnp.mean(x * x, axis=-1, keepdims=True) + RMS_EPS) * g_ref[...]
    z = _dot(h.astype(BF16), w_ref[...])

    def group(idx):
        return z[:, idx * GROUP:(idx + 1) * GROUP]

    def head_norm(zg, ones_ref, width, gain_row):
        ss = _dot((zg * zg).astype(BF16), ones_ref[...])
        return zg * lax.rsqrt(ss * (1.0 / width) + RMS_EPS) * gains_ref[gain_row:gain_row + 1, :]

    lane = lax.broadcasted_iota(jnp.int32, (tm, LANES), 1)

    def head_chunk(zg, hh):
        c = zg[:, (hh // 2) * LANES:(hh // 2 + 1) * LANES]
        if hh % 2:
            c = pltpu.roll(c, HEAD_DIM, axis=1)
        return c

    f = z[:, FORGET_COL:FORGET_COL + LANES] + bf_ref[...]
    ls = jnp.minimum(f, 0.0) - jnp.log1p(jnp.exp(-jnp.abs(f)))
    p0, p1, p2 = _split3(ls)
    tri = tri_ref[...]
    cum = (_dot(tri, p0) + _dot(tri, p1)) + _dot(tri, p2) + carry_ref[...]
    carry_ref[...] = cum[tm - 1:tm, :]

    fq = head_norm(group(0), ones64_ref, HEAD_DIM, 0)
    fk = head_norm(group(1), ones64_ref, HEAD_DIM, 1)
    fv = group(2)
    dq = head_norm(group(3), ones64_ref, HEAD_DIM, 2)
    dk = head_norm(group(4), ones64_ref, HEAD_DIM, 3)
    cq = head_norm(group(6), ones32_ref, DIFF_HALF, 4)
    ck = head_norm(group(7), ones32_ref, DIFF_HALF, 5)
    cv = group(8)

    for hf in range(2):
        dq_ref[0, hf] = dq[:, hf * LANES:(hf + 1) * LANES]
        dk_ref[0, hf] = dk[:, hf * LANES:(hf + 1) * LANES]
        dv_ref[0, hf] = group(5)[:, hf * LANES:(hf + 1) * LANES]
    su_ref[...] = group(9)

    one = jnp.ones((tm, LANES), F32)
    zero = jnp.zeros((tm, LANES), F32)
    for hh in range(N_HEADS):
        c = jnp.broadcast_to(cum[:, hh:hh + 1], (tm, LANES)) * LOG2E
        c0, c1, c2 = _split3(c)
        c0, c1, c2 = c0.astype(F32), c1.astype(F32), c2.astype(F32)
        qa = jnp.where(lane == 64, c0, jnp.where(lane == 65, c1, jnp.where(lane == 66, c2,
             jnp.where(lane < 70, one, zero))))
        ka = jnp.where(lane == 67, -c0, jnp.where(lane == 68, -c1, jnp.where(lane == 69, -c2,
             jnp.where(lane < 70, one, zero))))
        fq_ref[0, hh] = jnp.where(lane < HEAD_DIM, head_chunk(fq, hh), qa).astype(BF16)
        fk_ref[0, hh] = jnp.where(lane < HEAD_DIM, head_chunk(fk, hh), ka).astype(BF16)
        vaug = aug_ref[0:1, :]
        fv_ref[0, hh] = jnp.where(lane < HEAD_DIM, head_chunk(fv, hh), vaug).astype(BF16)
        qaug = aug_ref[1:2, :]
        cqc = head_chunk(cq, hh)
        cq1_ref[0, hh] = jnp.where(lane < DIFF_HALF, cqc, qaug).astype(BF16)
        cq2_ref[0, hh] = jnp.where((lane >= DIFF_HALF) & (lane < HEAD_DIM), cqc, qaug).astype(BF16)
        kaug = aug_ref[2 + hh:3 + hh, :]
        ck_ref[0, hh] = jnp.where(lane < HEAD_DIM, head_chunk(ck, hh), kaug).astype(BF16)
        cv_ref[0, hh] = jnp.where(lane < HEAD_DIM, head_chunk(cv, hh), vaug).astype(BF16)


def _inproj(x, g, w, bf, gains, ones64, ones32, tri, aug, tm):
    B, S, D = x.shape
    nt = S // tm
    hm = jax.ShapeDtypeStruct((B, N_HEADS, S, LANES), BF16)
    nat = jax.ShapeDtypeStruct((B, 2, S, LANES), F32)
    hm_spec = pl.BlockSpec((1, N_HEADS, tm, LANES), lambda b, i: (b, 0, i, 0))
    nat_spec = pl.BlockSpec((1, 2, tm, LANES), lambda b, i: (b, 0, i, 0))

    def full(a):
        return pl.BlockSpec(a.shape, lambda b, i: (0,) * a.ndim)

    return pl.pallas_call(
        _inproj_kernel,
        grid=(B, nt),
        in_specs=[pl.BlockSpec((1, tm, D), lambda b, i: (b, i, 0)),
                  full(g), full(w), full(bf), full(gains), full(ones64), full(ones32), full(tri), full(aug)],
        out_specs=[hm_spec] * 7 + [nat_spec] * 3 + [pl.BlockSpec((tm, GROUP), lambda b, i: (i, b))],
        out_shape=[hm] * 7 + [nat] * 3 + [jax.ShapeDtypeStruct((S, B * GROUP), F32)],
        scratch_shapes=[pltpu.VMEM((1, LANES), F32)],
        compiler_params=_cparams(("arbitrary", "arbitrary")),
        name="inproj",
    )(x, g, w, bf, gains, ones64, ones32, tri, aug)


def _flash_step(q, k, v, m, acc, bias=None):
    s = _dot_nt(q, k)
    if bias is not None:
        s = s + bias
    m_new = jnp.maximum(m, jnp.max(s, axis=1, keepdims=True))
    alpha = jnp.exp2(m - m_new)
    p = jnp.exp2(s - m_new)
    acc = alpha * acc + _dot(p.astype(BF16), v)
    return m_new, acc


def _fox_kernel(q_ref, k_ref, v_ref, o_ref, *, tq, tk, tkd):
    i = pl.program_id(2)
    q = q_ref[0, 0]

    def kv(off, size):
        return k_ref[0, 0, pl.ds(off, size), :], v_ref[0, 0, pl.ds(off, size), :]

    def body(j, carry):
        m, acc = carry
        k, v = kv(pl.multiple_of(j * tk, tk), tk)
        return _flash_step(q, k, v, m, acc)

    m0 = jnp.full((tq, 1), NEG, F32)
    acc0 = jnp.zeros((tq, LANES), F32)
    m, acc = lax.fori_loop(0, i * (tq // tk), body, (m0, acc0))
    for d in range(tq // tkd):
        rows = tq - d * tkd
        k, v = kv(pl.multiple_of(i * tq + d * tkd, tkd), tkd)
        row = lax.broadcasted_iota(jnp.int32, (rows, tkd), 0)
        col = lax.broadcasted_iota(jnp.int32, (rows, tkd), 1)
        mask = jnp.where(col <= row, 0.0, NEG)
        m_d, acc_d = _flash_step(q[d * tkd:], k, v, m[d * tkd:], acc[d * tkd:], mask)
        if d:
            m = jnp.concatenate([m[:d * tkd], m_d], axis=0)
            acc = jnp.concatenate([acc[:d * tkd], acc_d], axis=0)
        else:
            m, acc = m_d, acc_d
    den = jnp.broadcast_to(acc[:, HEAD_DIM:HEAD_DIM + 1], (tq, LANES))
    o_ref[0, 0] = (acc / den).astype(BF16)


def _fox_attention(q, k, v, tq, tk, tkd):
    B, H, S, _ = q.shape
    return pl.pallas_call(
        functools.partial(_fox_kernel, tq=tq, tk=tk, tkd=tkd),
        grid=(B, H, S // tq),
        in_specs=[pl.BlockSpec((1, 1, tq, LANES), lambda b, h, i: (b, h, i, 0)),
                  pl.BlockSpec((1, 1, S, LANES), lambda b, h, i: (b, h, 0, 0)),
                  pl.BlockSpec((1, 1, S, LANES), lambda b, h, i: (b, h, 0, 0))],
        out_specs=pl.BlockSpec((1, 1, tq, LANES), lambda b, h, i: (b, h, i, 0)),
        out_shape=jax.ShapeDtypeStruct((B, H, S, LANES), BF16),
        compiler_params=_cparams(("arbitrary", "arbitrary", "arbitrary")),
        name="fox_attention",
    )(q, k, v)


def _diff_kernel(lam_ref, q1_ref, q2_ref, k_ref, v_ref, bias_ref, gain_ref, o_ref, *, t, n_near):
    i = pl.program_id(2)
    q = jnp.concatenate([q1_ref[0, 0], q2_ref[0, 0]], axis=0)
    c = jnp.minimum(i + 1, n_near)
    n_far = i + 1 - c
    far_pairs = n_far // 2
    near_single = c % 2

    def step(carry, tile, tiles, bias=None):
        m, acc = carry
        off = pl.multiple_of(tile * t, t)
        k, v = k_ref[0, 0, pl.ds(off, tiles * t), :], v_ref[0, 0, pl.ds(off, tiles * t), :]
        if bias is not None:
            bias = jnp.concatenate([bias, bias], axis=0)
        return _flash_step(q, k, v, m, acc, bias)

    def near_pair(p, carry):
        first = c - 1 - near_single - 2 * p
        return step(carry, i - first, 2, bias_ref[0, (first - 1) // 2])

    m0 = jnp.full((2 * t, 1), NEG, F32)
    a0 = jnp.zeros((2 * t, LANES), F32)
    carry = lax.fori_loop(0, far_pairs, lambda j, cr: step(cr, 2 * j, 2), (m0, a0))
    carry = lax.fori_loop(0, n_far % 2, lambda _, cr: step(cr, n_far - 1, 1), carry)
    carry = lax.fori_loop(0, near_single, lambda _, cr: step(cr, n_far, 1, bias_ref[0, (c - 1) // 2, :, t:]), carry)
    _, acc = lax.fori_loop(0, c // 2, near_pair, carry)
    a1, a2 = acc[:t], acc[t:]
    d1 = jnp.broadcast_to(a1[:, HEAD_DIM:HEAD_DIM + 1], (t, LANES))
    d2 = jnp.broadcast_to(a2[:, HEAD_DIM:HEAD_DIM + 1], (t, LANES))
    lane = lax.broadcasted_iota(jnp.int32, (t, LANES), 1)
    o = jnp.where(lane < HEAD_DIM, a1 / d1 - lam_ref[0] * (a2 / d2), 0.0)
    ms = jnp.sum(o * o, axis=1, keepdims=True) * (1.0 / HEAD_DIM)
    o_ref[0, 0] = (o * lax.rsqrt(ms + RMS_EPS) * gain_ref[...] * lam_ref[1]).astype(BF16)


def _diff_attention(lam, q1, q2, k, v, bias, gain, t):
    B, H, S, _ = q1.shape
    n_near = 2 * bias.shape[1]
    qspec = pl.BlockSpec((1, 1, t, LANES), lambda b, h, i: (b, h, i, 0))
    kspec = pl.BlockSpec((1, 1, S, LANES), lambda b, h, i: (b, h, 0, 0))
    return pl.pallas_call(
        functools.partial(_diff_kernel, t=t, n_near=n_near),
        grid=(B, H, S // t),
        in_specs=[pl.BlockSpec(memory_space=pltpu.SMEM), qspec, qspec, kspec, kspec,
                  pl.BlockSpec((1, n_near // 2, t, 2 * t), lambda b, h, i: (h, 0, 0, 0)),
                  pl.BlockSpec((1, LANES), lambda b, h, i: (0, 0))],
        out_specs=qspec,
        out_shape=jax.ShapeDtypeStruct((B, H, S, LANES), BF16),
        compiler_params=_cparams(("arbitrary", "arbitrary", "arbitrary")),
        name="diff_attention",
    )(lam, q1, q2, k, v, bias, gain)


def _dil_kernel(qc_ref, kp_ref, kc_ref, vp_ref, vc_ref, b0_ref, b1_ref, b2_ref, o_ref, num_ref, st_ref):
    first = pl.program_id(1) == 0
    w = DIL_BLOCK
    lane = lax.broadcasted_iota(jnp.int32, (w, GROUP), 1)
    lane_s = lax.broadcasted_iota(jnp.int32, (w, LANES), 1)
    heads = [(lane >= hh * HEAD_DIM) & (lane < (hh + 1) * HEAD_DIM) for hh in range(N_HEADS)]
    ones = jnp.ones((w, LANES), BF16)

    def rows(cur_ref, prev_ref, off, res, dil):
        ref, base = (cur_ref, off) if off >= 0 else (prev_ref, off + DIL_SUPER)
        if dil == 1:
            parts = [ref[0, hf, base:base + w, :] for hf in range(2)]
        else:
            parts = [ref[0, hf, pl.ds(base + res, w, stride=dil), :] for hf in range(2)]
        return jnp.concatenate(parts, axis=1).astype(BF16)

    biases = (b0_ref, b1_ref, b2_ref)
    jobs = [(n, dil, u * w * dil, res) for n, (_, dil) in enumerate(DIL_PAIRS)
            for u in range(DIL_SUPER // (w * dil)) for res in range(dil)]
    for g in range(0, len(jobs), 2):
        group = jobs[g:g + 2]
        stage = []
        for n, dil, off, res in group:
            span = w * dil
            q = rows(qc_ref, qc_ref, off, res, dil)
            kc, vc = rows(kc_ref, kp_ref, off, res, dil), rows(vc_ref, vp_ref, off, res, dil)
            kp, vp = rows(kc_ref, kp_ref, off - span, res, dil), rows(vc_ref, vp_ref, off - span, res, dil)
            qs = jnp.concatenate([jnp.where(heads[hh], q, jnp.zeros_like(q)) for hh in range(N_HEADS)], axis=0)
            sp = _dot_nt(qs, kp) + biases[n][:, 0:w]
            if off == 0:
                sp = sp + jnp.where(first, NEG, 0.0)
            sc = _dot_nt(qs, kc) + biases[n][:, w:2 * w]
            stage.append((sp, sc, vp, vc))
        probs = []
        for sp, sc, vp, vc in stage:
            m = jnp.maximum(jnp.max(sp, axis=1, keepdims=True), jnp.max(sc, axis=1, keepdims=True))
            probs.append((m, jnp.exp(sp - m).astype(BF16), jnp.exp(sc - m).astype(BF16), vp, vc))
        outs = []
        for m, pp, pc, vp, vc in probs:
            den = _dot(pp, ones) + _dot(pc, ones)
            outs.append((m, den, _dot(pp, vp) + _dot(pc, vc)))
        for (n, dil, off, res), (m, den, o) in zip(group, outs):
            num = jnp.zeros((w, GROUP), F32)
            st = jnp.zeros((w, LANES), F32)
            for hh in range(N_HEADS):
                rs = slice(hh * w, (hh + 1) * w)
                num = jnp.where(heads[hh], o[rs], num)
                st = jnp.where(lane_s == hh, den[rs], jnp.where(lane_s == N_HEADS + hh, m[rs], st))
            dst = slice(off, off + w) if dil == 1 else pl.ds(off + res, w, stride=dil)
            for hf in range(2):
                num_ref[n, hf, dst, :] = num[:, hf * LANES:(hf + 1) * LANES]
            st_ref[n, dst, :] = st

    tm = 2 * w
    lane2 = lax.broadcasted_iota(jnp.int32, (tm, GROUP), 1)
    for c in range(DIL_SUPER // tm):
        sl = slice(c * tm, (c + 1) * tm)
        nums = [jnp.concatenate([num_ref[n, 0, sl, :], num_ref[n, 1, sl, :]], axis=1) for n in range(3)]
        sts = [st_ref[n, sl, :] for n in range(3)]
        out = jnp.zeros((tm, GROUP), F32)
        for hh in range(N_HEADS):
            ms = [s[:, N_HEADS + hh:N_HEADS + hh + 1] for s in sts]
            ds = [s[:, hh:hh + 1] for s in sts]
            mx = jnp.maximum(jnp.maximum(ms[0], ms[1]), ms[2])
            ws = [jnp.exp(m - mx) for m in ms]
            den = ws[0] * ds[0] + ws[1] * ds[1] + ws[2] * ds[2]
            o = (ws[0] * nums[0] + ws[1] * nums[1] + ws[2] * nums[2]) / den
            out = jnp.where((lane2 >= hh * HEAD_DIM) & (lane2 < (hh + 1) * HEAD_DIM), o, out)
        o_ref[0, sl, :] = out.astype(BF16)


def _dilated_attention(q, k, v, biases):
    B, _, S, _ = q.shape
    cur = pl.BlockSpec((1, 2, DIL_SUPER, LANES), lambda b, n: (b, 0, n, 0))
    prev = pl.BlockSpec((1, 2, DIL_SUPER, LANES), lambda b, n: (b, 0, jnp.maximum(n - 1, 0), 0))
    bspec = pl.BlockSpec(biases[0].shape, lambda b, n: (0, 0))
    return pl.pallas_call(
        _dil_kernel,
        grid=(B, S // DIL_SUPER),
        in_specs=[cur, prev, cur, prev, cur, bspec, bspec, bspec],
        out_specs=pl.BlockSpec((1, DIL_SUPER, GROUP), lambda b, n: (b, n, 0)),
        out_shape=jax.ShapeDtypeStruct((B, S, GROUP), BF16),
        scratch_shapes=[pltpu.VMEM((3, 2, DIL_SUPER, LANES), F32), pltpu.VMEM((3, DIL_SUPER, LANES), F32)],
        compiler_params=_cparams(("arbitrary", "arbitrary")),
        name="dilated_attention",
    )(q, k, k, v, v, *biases)


def _s5_kernel(u_ref, are_ref, aim_ref, bre_ref, bim_ref, cre_ref, cim_ref, d_ref, gw_ref, gb_ref,
               o_ref, xr_ref, xi_ref, bur_ref, bui_ref, sr_ref, si_ref, *, nb, tc):
    @pl.when(pl.program_id(0) == 0)
    def _():
        sr_ref[...] = jnp.zeros_like(sr_ref)
        si_ref[...] = jnp.zeros_like(si_ref)

    u = u_ref[...]
    ub = u.astype(BF16)
    bur_ref[...] = _dot(ub, bre_ref[...])
    bui_ref[...] = _dot(ub, bim_ref[...])
    ar = jnp.broadcast_to(are_ref[...], (nb, S5_WIDE))
    ai = jnp.broadcast_to(aim_ref[...], (nb, S5_WIDE))

    def step(t, carry):
        xr, xi = carry
        rows = pl.ds(pl.multiple_of(t * nb, nb), nb)
        nr = ar * xr - ai * xi + bur_ref[rows, :]
        ni = ar * xi + ai * xr + bui_ref[rows, :]
        xr_ref[rows, :] = nr
        xi_ref[rows, :] = ni
        return nr, ni

    xr, xi = lax.fori_loop(0, tc, step, (sr_ref[...], si_ref[...]))
    sr_ref[...] = xr
    si_ref[...] = xi
    y = (_dot(xr_ref[...].astype(BF16), cre_ref[...]) - _dot(xi_ref[...].astype(BF16), cim_ref[...])
         + d_ref[...] * u)
    g = _gelu(y)
    gate = _dot(g.astype(BF16), gw_ref[...]) + gb_ref[...]
    o_ref[...] = (g * (1.0 / (1.0 + jnp.exp(-gate)))).astype(BF16)


def _s5(u2, are, aim, bre, bim, cre, cim, d, gw, gb, nb, tc):
    rows = u2.shape[0]
    tr = tc * nb

    def full(a):
        return pl.BlockSpec(a.shape, lambda i: (0,) * a.ndim)

    consts = (are, aim, bre, bim, cre, cim, d, gw, gb)
    return pl.pallas_call(
        functools.partial(_s5_kernel, nb=nb, tc=tc),
        grid=(rows // tr,),
        in_specs=[pl.BlockSpec((tr, GROUP), lambda i: (i, 0))] + [full(a) for a in consts],
        out_specs=pl.BlockSpec((tr, GROUP), lambda i: (i, 0)),
        out_shape=jax.ShapeDtypeStruct((rows, GROUP), BF16),
        scratch_shapes=[pltpu.VMEM((tr, S5_WIDE), F32)] * 4 + [pltpu.VMEM((nb, S5_WIDE), F32)] * 2,
        compiler_params=_cparams(("arbitrary",)),
        name="s5_mixer",
    )(u2, *consts)


def _outproj_kernel(x_ref, fox_ref, dil_ref, dif_ref, s5_ref, wf_ref, wd_ref, wc_ref, ws_ref, o_ref):
    acc = x_ref[0] + _dot(dil_ref[0], wd_ref[...]) + _dot(s5_ref[...], ws_ref[...])
    for hh in range(N_HEADS):
        acc = acc + _dot(fox_ref[0, hh], wf_ref[hh]) + _dot(dif_ref[0, hh], wc_ref[hh])
    o_ref[0] = acc


def _outproj(x, fox, dil, dif, s5, wf, wd, wc, ws, tm):
    B, S, D = x.shape
    xspec = pl.BlockSpec((1, tm, D), lambda b, i: (b, i, 0))
    hspec = pl.BlockSpec((1, N_HEADS, tm, LANES), lambda b, i: (b, 0, i, 0))

    def full(a):
        return pl.BlockSpec(a.shape, lambda b, i: (0,) * a.ndim)

    return pl.pallas_call(
        _outproj_kernel,
        grid=(B, S // tm),
        in_specs=[xspec, hspec, pl.BlockSpec((1, tm, GROUP), lambda b, i: (b, i, 0)), hspec,
                  pl.BlockSpec((tm, GROUP), lambda b, i: (i, b)),
                  full(wf), full(wd), full(wc), full(ws)],
        out_specs=xspec,
        out_shape=jax.ShapeDtypeStruct((B, S, D), F32),
        compiler_params=_cparams(("arbitrary", "arbitrary")),
        name="outproj",
    )(x, fox, dil, dif, s5, wf, wd, wc, ws)


def _staircase(n):
    return [(a, b) for a in range(n) for b in range(n) if (a + 1) * (b + 1) <= n]


def _peer_select_kernel(x_ref, g_ref, wq_ref, keys_ref, h_ref, p1_ref, n1_ref, p2_ref, r2_ref):
    x = x_ref[...]
    tt = x.shape[0]
    hf = x * lax.rsqrt(jnp.mean(x * x, axis=-1, keepdims=True) + RMS_EPS) * g_ref[...]
    h = hf.astype(BF16)
    h_ref[...] = hf.T.astype(BF16)
    q = _dot(h, wq_ref[...]).astype(BF16)
    n_top = PEER_TOPK + 1
    pairs = _staircase(n_top)
    chains = [(hh, half) for hh in range(PEER_HEADS) for half in range(2)]
    scores = {}
    for hh, half in chains:
        c = (hh * 2 + half) * N_KEYS
        scores[hh, half] = _dot_nt(keys_ref[hh, half], q[:, c:c + N_KEYS])
    half_n = N_KEYS // 2
    swap, hi, lo = {}, {}, {}
    for ch in chains:
        a, b = scores[ch][:half_n], scores[ch][half_n:]
        swap[ch] = b > a
        hi[ch] = jnp.maximum(a, b)
        lo[ch] = jnp.minimum(a, b)
    tops = {ch: [] for ch in chains}
    for kk in range(n_top):
        for ch in chains:
            mx = jnp.max(hi[ch], axis=0, keepdims=True)
            tops[ch].append(mx)
            hit = hi[ch] == mx
            hi[ch] = jnp.where(hit, lo[ch], hi[ch])
            lo[ch] = jnp.where(hit, NEG * (1.0 + kk / 64.0), lo[ch])
    work = {}
    for hh in range(PEER_HEADS):
        ch = (hh, 1)
        both = hi[ch] < 0.5 * NEG
        one = lo[ch] < 0.5 * NEG
        big = jnp.where(both, hi[ch], jnp.where(one, lo[ch], 0.0))
        small = jnp.where(both, lo[ch], 0.0)
        work[ch] = jnp.concatenate([jnp.where(swap[ch], small, big),
                                    jnp.where(swap[ch], big, small)], axis=0)
    cands = [jnp.concatenate([tops[hh, 0][a] + tops[hh, 1][b] for a, b in pairs]
                             + [jnp.full((8 - len(pairs) % 8, tt), NEG, F32)], axis=0) for hh in range(PEER_HEADS)]
    bests = [[] for _ in range(PEER_HEADS)]
    for _ in range(n_top):
        for hh in range(PEER_HEADS):
            mx = jnp.max(cands[hh], axis=0, keepdims=True)
            bests[hh].append(mx)
            cands[hh] = jnp.where(cands[hh] == mx, NEG, cands[hh])
    for hh in range(PEER_HEADS):
        best = bests[hh]
        tops_h = [tops[hh, 0], tops[hh, 1]]
        scores_h = [scores[hh, 0], scores[hh, 1]]
        left = work[hh, 1]
        rank = jnp.where(left < 0.5 * NEG, (left * (1.0 / NEG) - 1.0) * 64.0, float(N_KEYS))
        z = jnp.zeros((1, tt), F32)
        for kk in range(PEER_TOPK):
            z = z + jnp.exp(best[kk] - best[0])
        cut = 0.5 * (best[PEER_TOPK - 1] + best[PEER_TOPK])
        n1 = jnp.zeros((N_KEYS, tt), F32)
        for b in range(PEER_TOPK):
            n1 = jnp.where(scores_h[0] >= cut - tops_h[1][b], b + 1.0, n1)
        p1_ref[hh] = jnp.exp(scores_h[0] - tops_h[0][0]) * (1.0 / z)
        n1_ref[hh] = n1
        p2_ref[hh] = pltpu.bitcast(jnp.exp(scores_h[1] - tops_h[1][0]).astype(BF16), jnp.uint32)
        r2_ref[hh] = pltpu.bitcast(rank.astype(BF16), jnp.uint32)


def _peer_select(x2, g, wq, keys, tt):
    T, D = x2.shape
    rows = pl.BlockSpec((PEER_HEADS, N_KEYS, tt), lambda i: (0, 0, i))
    words = pl.BlockSpec((PEER_HEADS, N_KEYS // 2, tt), lambda i: (0, 0, i))
    return pl.pallas_call(
        _peer_select_kernel,
        grid=(T // tt,),
        in_specs=[pl.BlockSpec((tt, D), lambda i: (i, 0)),
                  pl.BlockSpec(g.shape, lambda i: (0, 0)),
                  pl.BlockSpec(wq.shape, lambda i: (0, 0)),
                  pl.BlockSpec(keys.shape, lambda i: (0, 0, 0, 0))],
        out_specs=[pl.BlockSpec((D, tt), lambda i: (0, i)), rows, rows, words, words],
        out_shape=[jax.ShapeDtypeStruct((D, T), BF16),
                   jax.ShapeDtypeStruct((PEER_HEADS, N_KEYS, T), F32),
                   jax.ShapeDtypeStruct((PEER_HEADS, N_KEYS, T), F32),
                   jax.ShapeDtypeStruct((PEER_HEADS, N_KEYS // 2, T), jnp.uint32),
                   jax.ShapeDtypeStruct((PEER_HEADS, N_KEYS // 2, T), jnp.uint32)],
        compiler_params=_cparams(("arbitrary",)),
        name="peer_select",
    )(x2, g, wq, keys)


def _gelu_gate(x, g):
    ga, gb, one = (jnp.asarray(c, x.dtype) for c in (GELU_A, GELU_B, 1.0))
    e = jnp.exp2(x * (ga + gb * (x * x)))
    return x * g / (one + e)


def _peer_dense_kernel(x_ref, h_ref, p1_ref, n1_ref, p2_ref, r2_ref, u_ref, vt_ref, o_ref,
                       a_ref, w_ref, acc_ref, *, te, tt, n_e):
    s = pl.program_id(0)

    @pl.when(s == 0)
    def _():
        a_ref[...] = jnp.zeros_like(a_ref)
        w_ref[...] = jnp.zeros_like(w_ref)
        acc_ref[...] = jnp.zeros_like(acc_ref)

    slot_a = s % 2
    slot_b = 1 - slot_a
    j_b = jnp.maximum(s - 1, 0) % n_e
    j_c = jnp.maximum(s - 2, 0) % n_e

    h = h_ref[...]
    for q in range(te // PEER_SUB):
        a_ref[slot_a, q * PEER_SUB:(q + 1) * PEER_SUB, :] = _dot(
            u_ref[q * PEER_SUB:(q + 1) * PEER_SUB, :], h).astype(BF16)

    acc_ref[...] = jnp.where(j_c == 0, 0.0, acc_ref[...]) + _dot(vt_ref[...], w_ref[slot_a])

    rows_per_step = te // N_KEYS
    rows = pl.ds(pl.multiple_of(j_b * rows_per_step, rows_per_step), rows_per_step)
    for r in range(rows_per_step):
        for c in range(tt // LANES):
            cols = slice(c * LANES, (c + 1) * LANES)
            g = jnp.zeros((N_KEYS, LANES), BF16)
            for hh in range(PEER_HEADS):
                gate = jnp.broadcast_to(p1_ref[hh, rows, cols][r:r + 1], (N_KEYS, LANES)).astype(BF16)
                count = jnp.broadcast_to(n1_ref[hh, rows, cols][r:r + 1], (N_KEYS, LANES)).astype(BF16)
                r2 = pltpu.bitcast(r2_ref[hh, :, cols], BF16)
                p2 = pltpu.bitcast(p2_ref[hh, :, cols], BF16)
                sel = jnp.where(r2 < count, p2, jnp.zeros((), BF16))
                g = g + sel * gate
            a = a_ref[slot_b, r * N_KEYS:(r + 1) * N_KEYS, cols]
            w_ref[slot_b, r * N_KEYS:(r + 1) * N_KEYS, cols] = _gelu_gate(a, g)

    @pl.when((j_c == n_e - 1) & (s >= 2))
    def _():
        o_ref[...] = x_ref[...] + acc_ref[...].T


def _peer_dense(x2, h, p1, n1, p2, r2, u, vt, tt, te):
    T, D = x2.shape
    n_e = u.shape[0] // te
    last = (T // tt) * n_e - 1

    def tok(lag):
        return lambda s: jnp.clip(s - lag, 0, last) // n_e

    def tile(lag):
        return lambda s: jnp.clip(s - lag, 0, last) % n_e

    rows = pl.BlockSpec((PEER_HEADS, N_KEYS, tt), lambda s: (0, 0, tok(1)(s)))
    words = pl.BlockSpec((PEER_HEADS, N_KEYS // 2, tt), lambda s: (0, 0, tok(1)(s)))
    return pl.pallas_call(
        functools.partial(_peer_dense_kernel, te=te, tt=tt, n_e=n_e),
        grid=(last + 3,),
        in_specs=[pl.BlockSpec((tt, D), lambda s: (tok(2)(s), 0)),
                  pl.BlockSpec((D, tt), lambda s: (0, tok(0)(s))),
                  rows, rows, words, words,
                  pl.BlockSpec((te, D), lambda s: (tile(0)(s), 0)),
                  pl.BlockSpec((D, te), lambda s: (0, tile(2)(s)))],
        out_specs=pl.BlockSpec((tt, D), lambda s: (tok(2)(s), 0)),
        out_shape=jax.ShapeDtypeStruct((T, D), F32),
        scratch_shapes=[pltpu.VMEM((2, te, tt), BF16), pltpu.VMEM((2, te, tt), BF16), pltpu.VMEM((D, tt), F32)],
        compiler_params=_cparams(("arbitrary",)),
        name="peer_dense",
    )(x2, h, p1, n1, p2, r2, u, vt)


def _block_ones(width):
    idx = np.arange(GROUP) // width
    return jnp.asarray(idx[:, None] == idx[None, :], BF16)


def _pad_head_rows(w):
    w = w.reshape(N_HEADS, HEAD_DIM, -1)
    return jnp.concatenate([w, jnp.zeros_like(w)], axis=1).astype(BF16)


def _dil_bias(rel_bias, window, dil):
    ws = window // dil
    j = jnp.arange(DIL_BLOCK)[:, None]
    m = jnp.arange(2 * DIL_BLOCK)[None, :]
    sub = DIL_BLOCK + j - m
    band = (sub >= 0) & (sub <= ws)
    bias = rel_bias[_t5_bucket(dil * jnp.clip(sub, 0, ws))].transpose(2, 0, 1).astype(F32)
    return jnp.where(band[None], bias, NEG).reshape(N_HEADS * DIL_BLOCK, 2 * DIL_BLOCK)


def _diff_bias(rel_bias, t, n_near):
    dist = jnp.arange(-(t - 1), n_near * t + 2)
    b = rel_bias[_t5_bucket(dist)].astype(F32) - rel_bias[REL_BUCKETS - 1].astype(F32)
    b = jnp.where((dist >= 0)[:, None], b, NEG).T
    rev = b[:, ::-1]
    n = b.shape[1]
    tiles = []
    for delta in range(n_near):
        start = n - 1 - (delta * t + 2 * t - 2)
        u = jnp.pad(rev[:, start:start + 2 * t - 1], ((0, 0), (0, 1)))
        flat = jnp.tile(u, (1, t))[:, t - 1:t - 1 + t * (2 * t - 1)]
        tiles.append(flat.reshape(-1, t, 2 * t - 1)[:, :, :t])
    pairs = [jnp.concatenate([tiles[2 * p + 1], tiles[2 * p]], axis=-1) for p in range(n_near // 2)]
    return jnp.stack(pairs, axis=1)


def _s5_params(a_re, a_im, log_dt, b_re, b_im, c_re, c_im, d_skip):
    G, N, C = S5_GROUPS, S5_STATE, S5_CH
    dt = jnp.exp(log_dt.astype(F32))[:, None]
    ar, ai = a_re.astype(F32), a_im.astype(F32)
    mag = jnp.exp(dt * ar)
    abar_re, abar_im = mag * jnp.cos(dt * ai), mag * jnp.sin(dt * ai)
    den = ar * ar + ai * ai
    nr, ni = abar_re - 1.0, abar_im
    coef_re = (nr * ar + ni * ai) / den
    coef_im = (ni * ar - nr * ai) / den
    br, bi = b_re.astype(F32), b_im.astype(F32)
    bbar_re = coef_re[..., None] * br - coef_im[..., None] * bi
    bbar_im = coef_re[..., None] * bi + coef_im[..., None] * br
    eye = jnp.eye(G, dtype=F32)

    def b_blockdiag(bb):
        return jnp.einsum('gnc,gk->gckn', bb, eye).reshape(G * C, G * N).astype(BF16)

    def c_blockdiag(cc):
        return jnp.einsum('gcn,gk->gnkc', cc.astype(F32), eye).reshape(G * N, G * C).astype(BF16)

    return (abar_re.reshape(1, G * N), abar_im.reshape(1, G * N), b_blockdiag(bbar_re), b_blockdiag(bbar_im),
            c_blockdiag(c_re), c_blockdiag(c_im), d_skip.astype(F32).reshape(1, G * C))


def _pick(n, pref):
    for c in pref:
        if n % c == 0:
            return c
    raise ValueError("unsupported size %d" % n)


def kernel(x, w_in, b_forget, w_out, norm_mix, norm_ffn, fox_q_norm, fox_k_norm, dil_q_norm, dil_k_norm, diff_q_norm, diff_k_norm, diff_lambda, diff_out_norm, rel_bias, s5_a_re, s5_a_im, s5_log_dt, s5_b_re, s5_b_im, s5_c_re, s5_c_im, s5_d, s5_glu_w, s5_glu_b, peer_wq, peer_keys, peer_u, peer_v):
    B, S, D = x.shape
    depth = w_in.shape[0]
    T = B * S
    assert S % (DIL_BLOCK * DIL_PAIRS[-1][1]) == 0 and D % LANES == 0
    tm = _pick(S, (512, 256, 128))
    t_diff = _pick(S, (512, 256))
    tq_fox = _pick(S, (1024, 512, 256))
    tkd_fox = min(tq_fox, 512)
    tc = _pick(S, (128, 64))
    tt_sel = _pick(T, (512, 256))
    tt_peer = _pick(T, (512, 256))
    te = 2048

    n_sat = _bucket_saturation()
    n_near = -(-(n_sat + t_diff - 1) // t_diff)
    n_near = min(n_near + n_near % 2, S // t_diff)
    rel_bias = rel_bias.astype(F32)
    dil_bias = [_dil_bias(rel_bias[:, :N_HEADS], w, r) for w, r in DIL_PAIRS]
    diff_bias = _diff_bias(rel_bias[:, N_HEADS:], t_diff, n_near) * LOG2E
    far = rel_bias[REL_BUCKETS - 1, N_HEADS:] * LOG2E
    far_hi = far.astype(BF16).astype(F32)
    lane = jnp.arange(LANES)
    aug = jnp.zeros((2 + N_HEADS, LANES), F32)
    aug = aug.at[0].set(jnp.where(lane == HEAD_DIM, 1.0, 0.0))
    aug = aug.at[1].set(jnp.where((lane == HEAD_DIM) | (lane == HEAD_DIM + 1), 1.0, 0.0))
    aug = aug.at[2:].set(jnp.where(lane[None] == HEAD_DIM, far_hi[:, None],
                                   jnp.where(lane[None] == HEAD_DIM + 1, (far - far_hi)[:, None], 0.0)))
    ones64, ones32 = _block_ones(HEAD_DIM), _block_ones(DIFF_HALF)
    tri = jnp.asarray(np.tril(np.ones((tm, tm), np.float32)), BF16)

    sizes = [GROUP, GROUP, GROUP, N_HEADS] + [GROUP] * 7
    starts = np.cumsum([0] + sizes)
    order = [0, 1, 2, 4, 5, 6, 7, 8, 9, 10]

    x = x.astype(F32)
    for layer in range(depth):
        wl = w_in[layer]
        w_packed = jnp.concatenate(
            [wl[:, starts[k]:starts[k + 1]] for k in order]
            + [wl[:, starts[3]:starts[4]], jnp.zeros((D, LANES - N_HEADS), wl.dtype)], axis=1).astype(BF16)
        bf = jnp.zeros((1, LANES), F32).at[0, :N_HEADS].set(b_forget[layer].astype(F32))
        gains = jnp.stack([
            jnp.tile(fox_q_norm[layer].astype(F32), N_HEADS) * (HEAD_DIM ** -0.5 * LOG2E),
            jnp.tile(fox_k_norm[layer].astype(F32), N_HEADS),
            jnp.tile(dil_q_norm[layer].astype(F32), N_HEADS) * HEAD_DIM ** -0.5,
            jnp.tile(dil_k_norm[layer].astype(F32), N_HEADS),
            jnp.tile(diff_q_norm[layer].astype(F32), 2 * N_HEADS) * (DIFF_HALF ** -0.5 * LOG2E),
            jnp.tile(diff_k_norm[layer].astype(F32), 2 * N_HEADS),
            jnp.zeros((GROUP,), F32), jnp.zeros((GROUP,), F32)])
        (fq, fk, fv, cq1, cq2, ck, cv, dq, dk, dv, su) = _inproj(
            x, norm_mix[layer].astype(F32)[None], w_packed, bf, gains, ones64, ones32, tri, aug, tm)

        fox_o = _fox_attention(fq, fk, fv, tq_fox, tq_fox, tkd_fox)

        lam_init = 0.8 - 0.6 * math.exp(-0.3 * layer)
        lam_p = diff_lambda[layer].astype(F32)
        lam = jnp.exp(jnp.sum(lam_p[0] * lam_p[1])) - jnp.exp(jnp.sum(lam_p[2] * lam_p[3])) + lam_init
        out_gain = jnp.zeros((1, LANES), F32).at[0, :HEAD_DIM].set(diff_out_norm[layer].astype(F32))
        diff_o = _diff_attention(jnp.stack([lam, 1.0 - lam_init]).astype(F32), cq1, cq2, ck, cv, diff_bias,
                                 out_gain, t_diff)

        dil_o = _dilated_attention(dq, dk, dv, dil_bias)

        s5p = _s5_params(s5_a_re[layer], s5_a_im[layer], s5_log_dt[layer], s5_b_re[layer], s5_b_im[layer],
                         s5_c_re[layer], s5_c_im[layer], s5_d[layer])
        s5_o = _s5(su.reshape(S * B, GROUP), *s5p, s5_glu_w[layer].astype(BF16),
                   s5_glu_b[layer].astype(F32)[None], B, tc).reshape(S, B * GROUP)

        wo = w_out[layer]
        x = _outproj(x, fox_o, dil_o, diff_o, s5_o,
                     _pad_head_rows(wo[0:GROUP]), wo[GROUP:2 * GROUP].astype(BF16),
                     _pad_head_rows(wo[2 * GROUP:3 * GROUP]), wo[3 * GROUP:].astype(BF16), tm)

        x2 = x.reshape(T, D)
        h, p1, n1, p2, r2 = _peer_select(x2, norm_ffn[layer].astype(F32)[None], peer_wq[layer].astype(BF16),
                                         peer_keys[layer].astype(BF16), tt_sel)
        x = _peer_dense(x2, h, p1, n1, p2, r2, peer_u[layer].astype(BF16), peer_v[layer].astype(BF16).T,
                        tt_peer, te).reshape(B, S, D)
    return x
```
